```python
import math
import jax
import jax.numpy as jnp
from jax import lax
import numpy as np

D_MODEL = 1024
BATCH = 1
SEQ = 16384
DEPTH = 2

GRID_W = 64
CTX_LEN = 256
N_MOD = 9
D_FF = 2816
POOL_WINDOWS = (2, 4, 8, 16)
POOL_GROUP = D_MODEL // 16
POOL_W = POOL_GROUP * len(POOL_WINDOWS)
SSM_W = D_MODEL // 4
SSM_GROUP = 16
SSM_GROUPS = SSM_W // SSM_GROUP
SSM_STATE = 64
HEAD_DIM = 64
N_Q_HEADS = D_MODEL // 128
N_KV_HEADS = 2
KV_REP = N_Q_HEADS // N_KV_HEADS
Q_W = N_Q_HEADS * HEAD_DIM
KV_W = N_KV_HEADS * HEAD_DIM
QBLOCK = 128
ROPE_THETA = 10000.0
N_BRANCH = 3
IN_SPLITS = (POOL_W, POOL_W + SSM_W, POOL_W + SSM_W + Q_W,
             POOL_W + SSM_W + Q_W + KV_W, POOL_W + SSM_W + Q_W + 2 * KV_W)
IN_W = IN_SPLITS[-1] + N_BRANCH * D_MODEL
EPS = 1e-6

kernel_name = 'hybrid_pool_s5_gqa_prefix_block'


def rmsnorm(x, g):
    xf = x.astype(jnp.float32)
    xf = xf * lax.rsqrt(jnp.mean(xf * xf, axis=-1, keepdims=True) + EPS)
    return (xf * g.astype(jnp.float32)).astype(x.dtype)


def modulate(x, g, shift, scale):
    return rmsnorm(x, g) * (1 + scale) + shift


def adaln(cond_silu, w, b):
    return (cond_silu @ w + b).reshape(cond_silu.shape[0], 1, N_MOD, D_MODEL)


def swiglu(h, w13, w2):
    gt, up = jnp.split(h @ w13, 2, axis=-1)
    return (jax.nn.silu(gt) * up) @ w2


def multiscale_pool(a):
    n = a.shape[1]
    af = a.astype(jnp.float32)
    cs = jnp.concatenate([jnp.zeros_like(af[:, :1]), jnp.cumsum(af, axis=1)], axis=1)
    idx = jnp.arange(n)
    outs = []
    for gi, w in enumerate(POOL_WINDOWS):
        lo = jnp.clip(idx - w // 2, 0, n)
        hi = jnp.clip(idx + (w - 1 - w // 2) + 1, 0, n)
        csg = cs[..., gi * POOL_GROUP:(gi + 1) * POOL_GROUP]
        cnt = (hi - lo).astype(jnp.float32)[None, :, None]
        outs.append((csg[:, hi] - csg[:, lo]) / cnt)
    return (jnp.concatenate(outs, axis=-1) - af).astype(a.dtype)


def pool_branch(a, pool_w, pool_scale, pool_out):
    b, n, _ = a.shape
    y = multiscale_pool(a).reshape(b, n, len(POOL_WINDOWS), POOL_GROUP)
    y = jnp.einsum('bngc,gcd->bngd', y, pool_w).reshape(b, n, POOL_W)
    return (y * pool_scale) @ pool_out


def ssm_discretise(a_re, a_im, log_dt, b_re, b_im):
    lam = lax.complex(a_re.astype(jnp.float32), a_im.astype(jnp.float32))
    dt = jnp.exp(log_dt.astype(jnp.float32))[:, None]
    a_bar = jnp.exp(lam * dt)
    b_c = lax.complex(b_re.astype(jnp.float32), b_im.astype(jnp.float32))
    b_bar = ((a_bar - 1) / lam)[..., None] * b_c
    return a_bar, b_bar


def _lin_combine(left, right):
    a1, b1 = left
    a2, b2 = right
    return a1 * a2, a2 * b1 + b2


def ssm_states(u, a_bar, b_bar, s0):
    bu = lax.complex(jnp.einsum('bngh,gph->bngp', u, b_bar.real),
                     jnp.einsum('bngh,gph->bngp', u, b_bar.imag))
    a = jnp.broadcast_to(a_bar, bu.shape)
    a_cum, s = lax.associative_scan(_lin_combine, (a, bu), axis=1)
    if s0 is not None:
        s = s + a_cum * s0[:, None]
    return s


def ssm_readout(s, c_re, c_im):
    return jnp.einsum('bngp,ghp->bngh', s.real, c_re) - jnp.einsum('bngp,ghp->bngh', s.imag, c_im)


def _rev(t, flip):
    return jnp.flip(t, axis=1) if flip else t


def ssm_glu(y, w_glu):
    za, zb = jnp.split(jax.nn.gelu(y) @ w_glu, 2, axis=-1)
    return za * jax.nn.sigmoid(zb)


def ssm_branch(s_lat, s_ctx, a_re, a_im, log_dt, b_re, b_im, c_re, c_im, d_skip, w_glu, with_ctx_out):
    b, n, _ = s_lat.shape
    lc = s_ctx.shape[1]
    ul = s_lat.astype(jnp.float32).reshape(b, n, SSM_GROUPS, SSM_GROUP)
    uc = s_ctx.astype(jnp.float32).reshape(b, lc, SSM_GROUPS, SSM_GROUP)
    d = d_skip.astype(jnp.float32).reshape(SSM_GROUPS, SSM_GROUP)
    y_lat = ul * d
    y_ctx = uc * d
    for direction in range(2):
        flip = direction == 1
        a_bar, b_bar = ssm_discretise(a_re[direction], a_im[direction], log_dt[direction],
                                      b_re[direction], b_im[direction])
        cr = c_re[direction].astype(jnp.float32)
        ci = c_im[direction].astype(jnp.float32)
        sc = ssm_states(_rev(uc, flip), a_bar, b_bar, None)
        sl = ssm_states(_rev(ul, flip), a_bar, b_bar, sc[:, -1])
        y_lat = y_lat + _rev(ssm_readout(sl, cr, ci), flip)
        if with_ctx_out:
            y_ctx = y_ctx + _rev(ssm_readout(sc, cr, ci), flip)
    out_lat = ssm_glu(y_lat.reshape(b, n, SSM_W).astype(s_lat.dtype), w_glu)
    out_ctx = ssm_glu(y_ctx.reshape(b, lc, SSM_W).astype(s_ctx.dtype), w_glu) if with_ctx_out else None
    return out_lat, out_ctx


def _rope_axis(x, pos):
    r = x.shape[-1]
    inv = ROPE_THETA ** (-jnp.arange(0, r, 2, dtype=jnp.float32) / r)
    ang = pos.astype(jnp.float32)[:, None] * inv[None]
    cos = jnp.cos(ang)[None, :, None]
    sin = jnp.sin(ang)[None, :, None]
    x1, x2 = jnp.split(x, 2, axis=-1)
    return jnp.concatenate([x1 * cos - x2 * sin, x2 * cos + x1 * sin], axis=-1)


def rope_2d(t, rows, cols):
    half = HEAD_DIM // 2
    tf = t.astype(jnp.float32)
    out = jnp.concatenate([_rope_axis(tf[..., :half], rows), _rope_axis(tf[..., half:], cols)], axis=-1)
    return out.astype(t.dtype)


def attend_blocks(q, k, v):
    b, n = q.shape[:2]
    nb = n // QBLOCK
    qb = q.reshape(b, nb, QBLOCK, N_KV_HEADS, KV_REP, HEAD_DIM).transpose(1, 0, 2, 3, 4, 5)
    scale = HEAD_DIM ** -0.5

    def one_block(qblk):
        s = jnp.einsum('bqkrd,bmkd->bkrqm', qblk, k, preferred_element_type=jnp.float32) * scale
        p = jax.nn.softmax(s, axis=-1).astype(v.dtype)
        return jnp.einsum('bkrqm,bmkd->bqkrd', p, v)

    o = lax.map(one_block, qb)
    return o.transpose(1, 0, 2, 3, 4, 5).reshape(b, n, Q_W)


def gated_merge(gates, br_a, br_b, br_c, w_out):
    g = jax.nn.sigmoid(gates.astype(jnp.float32)).astype(br_a.dtype)
    g = g.reshape(*gates.shape[:-1], N_BRANCH, D_MODEL)
    mixed = g[..., 0, :] * br_a + g[..., 1, :] * br_b + g[..., 2, :] * br_c
    return mixed @ w_out


def token_mixer(h, hc, rows, cols, w_in, pool_w, pool_scale, pool_out,
                a_re, a_im, log_dt, b_re, b_im, c_re, c_im, d_skip, w_glu,
                q_g, k_g, attn_out, w_out, with_ctx_out):
    b, n, _ = h.shape
    lc = hc.shape[1]
    a, s, q, k, v, gates = jnp.split(h @ w_in, list(IN_SPLITS), axis=-1)
    ac, sc, qc, kc, vc, gates_c = jnp.split(hc @ w_in, list(IN_SPLITS), axis=-1)
    br_a = pool_branch(a, pool_w, pool_scale, pool_out)
    br_b, br_b_c = ssm_branch(s, sc, a_re, a_im, log_dt, b_re, b_im, c_re, c_im, d_skip, w_glu, with_ctx_out)
    ql = rope_2d(rmsnorm(q.reshape(b, n, N_Q_HEADS, HEAD_DIM), q_g), rows, cols)
    kl = rope_2d(rmsnorm(k.reshape(b, n, N_KV_HEADS, HEAD_DIM), k_g), rows, cols)
    vl = v.reshape(b, n, N_KV_HEADS, HEAD_DIM)
    kcc = rmsnorm(kc.reshape(b, lc, N_KV_HEADS, HEAD_DIM), k_g)
    vcc = vc.reshape(b, lc, N_KV_HEADS, HEAD_DIM)
    k_all = jnp.concatenate([kl, kcc], axis=1)
    v_all = jnp.concatenate([vl, vcc], axis=1)
    br_c = attend_blocks(ql, k_all, v_all) @ attn_out
    out = gated_merge(gates, br_a, br_b, br_c, w_out)
    if not with_ctx_out:
        return out, None
    br_a_c = pool_branch(ac, pool_w, pool_scale, pool_out)
    qcc = rmsnorm(qc.reshape(b, lc, N_Q_HEADS, HEAD_DIM), q_g)
    br_c_c = attend_blocks(qcc, kcc, vcc) @ attn_out
    out_c = gated_merge(gates_c, br_a_c, br_b_c, br_c_c, w_out)
    return out, out_c


def setup_inputs(seed: int = 0) -> dict:
    key = jax.random.key(seed)
    ks = jax.random.split(key, 27)

    def nrm(k, shape, scale):
        return jax.random.normal(k, shape, jnp.float32) * scale

    G, P, H = SSM_GROUPS, SSM_STATE, SSM_GROUP
    n_idx = jnp.arange(P, dtype=jnp.float32)
    return {
        'x': nrm(ks[0], (BATCH, SEQ, D_MODEL), 1.0),
        'c': nrm(ks[1], (BATCH, D_MODEL), 1.0),
        'ctx': nrm(ks[2], (BATCH, CTX_LEN, D_MODEL), 1.0),
        'c_ctx': nrm(ks[3], (D_MODEL,), 1.0),
        'mod_w': nrm(ks[4], (DEPTH, D_MODEL, N_MOD * D_MODEL), 0.5 * D_MODEL ** -0.5),
        'mod_b': nrm(ks[5], (DEPTH, N_MOD * D_MODEL), 0.01),
        'norm_g': 1.0 + nrm(ks[6], (DEPTH, 3, D_MODEL), 0.02),
        'ffn_w13': nrm(ks[7], (DEPTH, 2, D_MODEL, 2 * D_FF), D_MODEL ** -0.5),
        'ffn_w2': nrm(ks[8], (DEPTH, 2, D_FF, D_MODEL), D_FF ** -0.5),
        'w_in': nrm(ks[9], (DEPTH, D_MODEL, IN_W), D_MODEL ** -0.5),
        'pool_w': nrm(ks[10], (DEPTH, len(POOL_WINDOWS), POOL_GROUP, POOL_GROUP), POOL_GROUP ** -0.5),
        'pool_scale': 1.0 + nrm(ks[11], (DEPTH, POOL_W), 0.02),
        'pool_out': nrm(ks[12], (DEPTH, POOL_W, D_MODEL), POOL_W ** -0.5),
        'ssm_a_re': -0.5 + nrm(ks[13], (DEPTH, 2, G, P), 0.01),
        'ssm_a_im': math.pi * n_idx + nrm(ks[14], (DEPTH, 2, G, P), 0.01),
        'ssm_log_dt': jax.random.uniform(ks[15], (DEPTH, 2, G), jnp.float32, math.log(1e-3), math.log(1e-1)),
        'ssm_b_re': nrm(ks[16], (DEPTH, 2, G, P, H), (2 * H) ** -0.5),
        'ssm_b_im': nrm(ks[17], (DEPTH, 2, G, P, H), (2 * H) ** -0.5),
        'ssm_c_re': nrm(ks[18], (DEPTH, 2, G, H, P), P ** -0.5),
        'ssm_c_im': nrm(ks[19], (DEPTH, 2, G, H, P), P ** -0.5),
        'ssm_d': nrm(ks[20], (DEPTH, SSM_W), 1.0),
        'ssm_glu_w': nrm(ks[21], (DEPTH, SSM_W, 2 * D_MODEL), SSM_W ** -0.5),
        'q_norm_g': 1.0 + nrm(ks[22], (DEPTH, HEAD_DIM), 0.02),
        'k_norm_g': 1.0 + nrm(ks[23], (DEPTH, HEAD_DIM), 0.02),
        'attn_out': nrm(ks[24], (DEPTH, Q_W, D_MODEL), Q_W ** -0.5),
        'w_out': nrm(ks[25], (DEPTH, D_MODEL, D_MODEL), D_MODEL ** -0.5),
        'final_norm_g': 1.0 + nrm(ks[26], (D_MODEL,), 0.02),
    }


def reference(x, c, ctx, c_ctx, mod_w, mod_b, norm_g, ffn_w13, ffn_w2, w_in,
              pool_w, pool_scale, pool_out, ssm_a_re, ssm_a_im, ssm_log_dt,
              ssm_b_re, ssm_b_im, ssm_c_re, ssm_c_im, ssm_d, ssm_glu_w,
              q_norm_g, k_norm_g, attn_out, w_out, final_norm_g):
    n = x.shape[1]
    n_rows = n // GRID_W
    rows = jnp.repeat(jnp.arange(n_rows), GRID_W)
    cols = jnp.broadcast_to(jnp.arange(GRID_W), (n_rows, GRID_W)).reshape(-1)
    cs_lat = jax.nn.silu(c)
    cs_ctx = jax.nn.silu(c_ctx)[None]
    for l in range(DEPTH):
        with_ctx_out = l < DEPTH - 1
        m = adaln(cs_lat, mod_w[l], mod_b[l])
        mc = adaln(cs_ctx, mod_w[l], mod_b[l])
        x = x + 0.5 * m[:, :, 2] * swiglu(modulate(x, norm_g[l, 0], m[:, :, 0], m[:, :, 1]),
                                          ffn_w13[l, 0], ffn_w2[l, 0])
        ctx = ctx + 0.5 * mc[:, :, 2] * swiglu(modulate(ctx, norm_g[l, 0], mc[:, :, 0], mc[:, :, 1]),
                                               ffn_w13[l, 0], ffn_w2[l, 0])
        h = modulate(x, norm_g[l, 1], m[:, :, 3], m[:, :, 4])
        hc = modulate(ctx, norm_g[l, 1], mc[:, :, 3], mc[:, :, 4])
        mix, mix_c = token_mixer(h, hc, rows, cols, w_in[l], pool_w[l], pool_scale[l], pool_out[l],
                                 ssm_a_re[l], ssm_a_im[l], ssm_log_dt[l], ssm_b_re[l], ssm_b_im[l],
                                 ssm_c_re[l], ssm_c_im[l], ssm_d[l], ssm_glu_w[l],
                                 q_norm_g[l], k_norm_g[l], attn_out[l], w_out[l], with_ctx_out)
        x = x + m[:, :, 5] * mix
        x = x + 0.5 * m[:, :, 8] * swiglu(modulate(x, norm_g[l, 2], m[:, :, 6], m[:, :, 7]),
                                          ffn_w13[l, 1], ffn_w2[l, 1])
        if with_ctx_out:
            ctx = ctx + mc[:, :, 5] * mix_c
            ctx = ctx + 0.5 * mc[:, :, 8] * swiglu(modulate(ctx, norm_g[l, 2], mc[:, :, 6], mc[:, :, 7]),
                                                   ffn_w13[l, 1], ffn_w2[l, 1])
    return rmsnorm(x, final_norm_g)
```

```python
import functools
import math

import jax
import jax.numpy as jnp
from jax import lax
from jax.experimental import pallas as pl
from jax.experimental.pallas import tpu as pltpu

F32 = jnp.float32
BF16 = jnp.bfloat16

D_MODEL = 1024
N_MOD = 9
D_FF = 2816
GRID_W = 64
POOL_GROUP = 64
POOL_W = 256
SSM_W = 256
SSM_GROUP = 16
SSM_GROUPS = 16
SSM_STATE = 64
HEAD_DIM = 64
N_Q_HEADS = 8
N_KV_HEADS = 2
KV_REP = N_Q_HEADS // N_KV_HEADS
Q_W = N_Q_HEADS * HEAD_DIM
KV_W = N_KV_HEADS * HEAD_DIM
ROPE_THETA = 10000.0
EPS = 1e-6
N_PROJ = POOL_W + SSM_W + Q_W + 2 * KV_W
QK_W = Q_W + KV_W

LANES = 128
SUBLANES = 8
VMEM_LIMIT_BYTES = 56 * 1024 * 1024

FFN_CHUNK = 256
SSM_CHUNK = 64
SSM_LH = SSM_CHUNK * SSM_GROUP
POOL_HALO = 8
Q_SCALE = HEAD_DIM ** -0.5 * math.log2(math.e)


def _row_tile(n):
    for t in (512, 256):
        if n % t == 0:
            return t
    raise ValueError(f"row count {n} must be a multiple of 256")


def _const_spec(shape):
    nd = len(shape)
    return pl.BlockSpec(shape, lambda *_: (0,) * nd, pipeline_mode=pl.Buffered(1))


def _params(*sem):
    return pltpu.CompilerParams(dimension_semantics=sem, vmem_limit_bytes=VMEM_LIMIT_BYTES)


def _modulate(x, g, shift, scale):
    ms = jnp.mean(x * x, axis=-1, keepdims=True)
    xn = x * lax.rsqrt(ms + EPS) * g
    return xn * (1.0 + scale) + shift


def _split2(x):
    hi = x.astype(BF16)
    lo = (x - hi.astype(F32)).astype(BF16)
    return hi, lo


def _dot(a, b):
    return jnp.dot(a, b, preferred_element_type=F32)


def _dot_sel(x, sel):
    x0 = x.astype(BF16)
    r1 = x - x0.astype(F32)
    x1 = r1.astype(BF16)
    x2 = (r1 - x1.astype(F32)).astype(BF16)
    return _dot(x0, sel) + _dot(x1, sel) + _dot(x2, sel)


def _dot_f32(a, b):
    ah, al = _split2(a)
    bh, bl = _split2(b)
    return _dot(ah, bh) + _dot(ah, bl) + _dot(al, bh)


def _adaln_kernel(c_ref, w_ref, b_ref, o_ref):
    c = c_ref[...]
    cs = c * jax.nn.sigmoid(c)
    o_ref[...] = _dot_f32(cs, w_ref[...]) + b_ref[...]


def _adaln(cond8, mod_w, mod_b):
    depth, d, nm = mod_w.shape
    tn = 1024
    return pl.pallas_call(
        _adaln_kernel,
        grid=(depth, nm // tn),
        in_specs=[
            pl.BlockSpec((SUBLANES, d), lambda l, j: (0, 0)),
            pl.BlockSpec((None, d, tn), lambda l, j: (l, 0, j)),
            pl.BlockSpec((None, 1, tn), lambda l, j: (l, 0, j)),
        ],
        out_specs=pl.BlockSpec((None, SUBLANES, tn), lambda l, j: (l, 0, j)),
        out_shape=jax.ShapeDtypeStruct((depth, SUBLANES, nm), F32),
        compiler_params=_params("arbitrary", "arbitrary"),
        name="adaln",
    )(cond8, mod_w, mod_b.reshape(depth, 1, nm))


def _ffn_kernel(x_ref, mod_ref, g_ref, w13_ref, w2_ref, *rest, final):
    o_ref = rest[-1]
    x = x_ref[...]
    h = _modulate(x, g_ref[...], mod_ref[0:1, :], mod_ref[1:2, :]).astype(BF16)
    acc = None
    for c in range(D_FF // FFN_CHUNK):
        gu = _dot(h, w13_ref[:, c * 2 * FFN_CHUNK:(c + 1) * 2 * FFN_CHUNK])
        g = gu[:, :FFN_CHUNK]
        u = gu[:, FFN_CHUNK:]
        act = (g * jax.nn.sigmoid(g) * u).astype(BF16)
        part = _dot(act, w2_ref[c * FFN_CHUNK:(c + 1) * FFN_CHUNK, :])
        acc = part if acc is None else acc + part
    y = x + (0.5 * mod_ref[2:3, :]) * acc
    if final:
        fg_ref = rest[0]
        y = y * lax.rsqrt(jnp.mean(y * y, axis=-1, keepdims=True) + EPS) * fg_ref[...]
    o_ref[...] = y


def _ffn(x, mod3, g, w13, w2, final_g=None):
    n, d = x.shape
    tm = _row_tile(n)
    final = final_g is not None
    in_specs = [
        pl.BlockSpec((tm, d), lambda i: (i, 0)),
        _const_spec((3, d)),
        _const_spec((1, d)),
        _const_spec(w13.shape),
        _const_spec(w2.shape),
    ]
    args = [x, mod3, g.reshape(1, d), w13, w2]
    if final:
        in_specs.append(_const_spec((1, d)))
        args.append(final_g.reshape(1, d))
    return pl.pallas_call(
        functools.partial(_ffn_kernel, final=final),
        grid=(n // tm,),
        in_specs=in_specs,
        out_specs=pl.BlockSpec((tm, d), lambda i: (i, 0)),
        out_shape=jax.ShapeDtypeStruct((n, d), F32),
        compiler_params=_params("arbitrary"),
        name="ffn_final" if final else "ffn",
    )(*args)


def _inproj_kernel(x_ref, mod_ref, g_ref, w_ref, seg_ref, gv_ref, *rest, rope):
    if rope:
        cos_ref, sa_ref, sb_ref = rest[:3]
        rest = rest[3:]
    a_ref, s_ref, q_ref, k_ref, v_ref = rest
    x = x_ref[...]
    h = _modulate(x, g_ref[...], mod_ref[0:1, :], mod_ref[1:2, :]).astype(BF16)
    proj = _dot(h, w_ref[...])
    a_ref[...] = proj[:, 0:POOL_W]
    s_ref[...] = proj[:, POOL_W:POOL_W + SSM_W]
    qk = proj[:, POOL_W + SSM_W:POOL_W + SSM_W + QK_W]
    hi, lo = _split2(qk * qk)
    seg = seg_ref[...]
    ss = _dot(hi, seg) + _dot(lo, seg)
    qkn = qk * lax.rsqrt(ss * (1.0 / HEAD_DIM) + EPS) * gv_ref[...]
    blocks = []
    for b in range(QK_W // LANES):
        blk = qkn[:, b * LANES:(b + 1) * LANES]
        if rope:
            blk = (blk * cos_ref[...] + pltpu.roll(blk, LANES - 16, 1) * sa_ref[...]
                   + pltpu.roll(blk, 16, 1) * sb_ref[...])
        blocks.append(blk.astype(BF16))
    q_ref[...] = jnp.concatenate(blocks[:Q_W // LANES], axis=-1)
    k_ref[...] = blocks[Q_W // LANES]
    v_ref[...] = proj[:, N_PROJ - KV_W:N_PROJ].astype(BF16)


def _inproj(x, mod2, g, w_p, seg, gv, rope_tabs=None):
    n, d = x.shape
    tm = _row_tile(n)
    rope = rope_tabs is not None
    in_specs = [
        pl.BlockSpec((tm, d), lambda i: (i, 0)),
        _const_spec((2, d)),
        _const_spec((1, d)),
        _const_spec(w_p.shape),
        _const_spec(seg.shape),
        _const_spec((1, QK_W)),
    ]
    args = [x, mod2, g.reshape(1, d), w_p, seg, gv]
    if rope:
        in_specs += [pl.BlockSpec((tm, LANES), lambda i: (i, 0))] * 3
        args += list(rope_tabs)
    widths = (POOL_W, SSM_W, Q_W, KV_W, KV_W)
    dtypes = (F32, F32, BF16, BF16, BF16)
    return pl.pallas_call(
        functools.partial(_inproj_kernel, rope=rope),
        grid=(n // tm,),
        in_specs=in_specs,
        out_specs=[pl.BlockSpec((tm, w), lambda i: (i, 0)) for w in widths],
        out_shape=[jax.ShapeDtypeStruct((n, w), dt) for w, dt in zip(widths, dtypes)],
        compiler_params=_params("arbitrary"),
        name="inproj_rope" if rope else "inproj",
    )(*args)


def _rope_tables(n):
    t = jnp.arange(n)
    rows = (t // GRID_W).astype(F32)
    cols = (t % GRID_W).astype(F32)
    half = HEAD_DIM // 2
    inv = ROPE_THETA ** (-jnp.arange(0, half, 2, dtype=F32) / half)
    lane = jnp.arange(LANES)
    dh = lane % HEAD_DIM
    e = dh % half
    pos = jnp.where((dh < half)[None, :], rows[:, None], cols[:, None])
    ang = pos * inv[e % (half // 2)][None, :]
    first = (e < half // 2)[None, :]
    sin = jnp.sin(ang)
    return jnp.cos(ang), jnp.where(first, -sin, 0.0), jnp.where(first, 0.0, sin)


def _ssm_kernel(u_ref, pc_ref, pr_ref, bc_ref, bt_ref, ct_ref, d_ref, y_ref,
                t_ref, w1_ref, w3_ref, sl_ref, sin_ref, *, n_ctx_chunks):
    L, H, P, LH = SSM_CHUNK, SSM_GROUP, SSM_STATE, SSM_LH
    nc = u_ref.shape[0]
    pc = pc_ref[...]
    pr = pr_ref[...]

    def powers(o):
        dt = jnp.exp(pc[:, o + 2:o + 3])
        zr = pc[:, o:o + 1] * dt
        zi = pc[:, o + 1:o + 2] * dt
        m = lax.broadcasted_iota(jnp.int32, (1, LANES), 1).astype(F32)
        mag = jnp.exp(zr * m)
        return mag * jnp.cos(zi * m), mag * jnp.sin(zi * m)

    def zoh(lr, li, ldt):
        dt = jnp.exp(ldt)
        mag = jnp.exp(lr * dt)
        nr = mag * jnp.cos(li * dt) - 1.0
        ni = mag * jnp.sin(li * dt)
        den = lr * lr + li * li
        return (nr * lr + ni * li) / den, (ni * lr - nr * li) / den

    def sel_mat(expo):
        w = expo.shape[1]
        return (lax.broadcasted_iota(jnp.int32, (LANES, w), 0) == expo).astype(BF16)

    def chan_sel(w):
        lane = lax.broadcasted_iota(jnp.int32, (H, w), 1)
        return (lane % H == lax.broadcasted_iota(jnp.int32, (H, w), 0)).astype(BF16)

    def cmul(ar, ai, br, bi):
        return ar * br - ai * bi, ar * bi + ai * br

    tok_lh = lax.broadcasted_iota(jnp.int32, (1, LH), 1) // H
    lag2 = lax.broadcasted_iota(jnp.int32, (1, 2 * LH), 1) // H - (L - 1)
    sel16_lh = chan_sel(LH)
    sel16_2lh = chan_sel(2 * LH)

    w1_ref[...] = jnp.zeros_like(w1_ref)
    w3_ref[...] = jnp.zeros_like(w3_ref)
    klong = None
    a_chunk = []
    for dr in range(2):
        o = 3 * dr
        apr, api = powers(o)
        qr, qi = zoh(pr[o:o + 1, :], pr[o + 1:o + 2, :], pr[o + 2:o + 3, :])
        bbr_t, bbi_t = cmul(qr, qi, bt_ref[2 * dr], bt_ref[2 * dr + 1])
        lag = lag2 if dr == 0 else -lag2
        sel = sel_mat(lag)
        er, ei = _dot_sel(apr, sel), _dot_sel(api, sel)
        cr = _dot_sel(ct_ref[2 * dr], sel16_2lh)
        ci = _dot_sel(ct_ref[2 * dr + 1], sel16_2lh)
        gr, gi = cmul(er, ei, cr, ci)
        kd = _dot_f32(bbr_t, gr) - _dot_f32(bbi_t, gi)
        klong = kd if klong is None else klong + kd
        qrc, qic = zoh(pc[:, o:o + 1], pc[:, o + 1:o + 2], pc[:, o + 2:o + 3])
        bbr, bbi = cmul(qrc, qic, bc_ref[2 * dr], bc_ref[2 * dr + 1])
        sel = sel_mat((L - 1 - tok_lh) if dr == 0 else tok_lh)
        er, ei = _dot_sel(apr, sel), _dot_sel(api, sel)
        wr, wi = cmul(er, ei, _dot_sel(bbr, sel16_lh), _dot_sel(bbi, sel16_lh))
        w1_ref[(4 * dr) * P:(4 * dr + 1) * P, :] = wr.astype(BF16)
        w1_ref[(4 * dr + 2) * P:(4 * dr + 3) * P, :] = wi.astype(BF16)
        sel = sel_mat((tok_lh + 1) if dr == 0 else (L - tok_lh))
        er, ei = _dot_sel(apr, sel), _dot_sel(api, sel)
        gr, gi = cmul(er, ei, _dot_sel(ct_ref[2 * dr], sel16_lh), _dot_sel(ct_ref[2 * dr + 1], sel16_lh))
        w3_ref[(4 * dr) * P:(4 * dr + 1) * P, :] = gr.astype(BF16)
        w3_ref[(4 * dr + 2) * P:(4 * dr + 3) * P, :] = (-gi).astype(BF16)
        dtr = jnp.exp(pr[o + 2:o + 3, :])
        mag = jnp.exp(pr[o:o + 1, :] * dtr * L)
        ang = pr[o + 1:o + 2, :] * dtr * L
        pad = jnp.zeros((1, LANES - P), F32)
        a_chunk.append((jnp.concatenate([mag * jnp.cos(ang), pad], axis=1),
                        jnp.concatenate([mag * jnp.sin(ang), pad], axis=1)))

    for j in range(L):
        start = (L - 1 - j) * H
        t_ref[j * H:(j + 1) * H, :] = klong[:, start:start + LH].astype(BF16)

    u = u_ref[...]
    ub = u.astype(BF16)
    ncl = nc - n_ctx_chunks
    sl_ref[ncl:, :] = jnp.zeros((SUBLANES, 4 * LANES), F32)
    sl_ref[0:nc, :] = lax.dot_general(ub, w1_ref[...], (((1,), (1,)), ((), ())), preferred_element_type=F32)

    (afr, afi), (abr, abi) = a_chunk
    rid = lax.broadcasted_iota(jnp.int32, (SUBLANES, LANES), 0)

    def block_scan(carry, rf0, rb0, n_rows):
        fr, fi, br, bi = carry
        lf_r, lf_i = sl_ref[pl.ds(rf0, SUBLANES), 0:LANES], sl_ref[pl.ds(rf0, SUBLANES), LANES:2 * LANES]
        lb_r = sl_ref[pl.ds(rb0, SUBLANES), 2 * LANES:3 * LANES]
        lb_i = sl_ref[pl.ds(rb0, SUBLANES), 3 * LANES:4 * LANES]
        of_r = of_i = ob_r = ob_i = jnp.zeros((SUBLANES, LANES), F32)
        for k in range(n_rows):
            r = k
            of_r = jnp.where(rid == r, fr, of_r)
            of_i = jnp.where(rid == r, fi, of_i)
            fr, fi = (afr * fr - afi * fi + lf_r[r:r + 1, :], afr * fi + afi * fr + lf_i[r:r + 1, :])
            r = n_rows - 1 - k
            ob_r = jnp.where(rid == r, br, ob_r)
            ob_i = jnp.where(rid == r, bi, ob_i)
            br, bi = (abr * br - abi * bi + lb_r[r:r + 1, :], abr * bi + abi * br + lb_i[r:r + 1, :])
        sin_ref[pl.ds(rf0, SUBLANES), 0:LANES] = of_r
        sin_ref[pl.ds(rf0, SUBLANES), LANES:2 * LANES] = of_i
        sin_ref[pl.ds(rb0, SUBLANES), 2 * LANES:3 * LANES] = ob_r
        sin_ref[pl.ds(rb0, SUBLANES), 3 * LANES:4 * LANES] = ob_i
        return fr, fi, br, bi

    z = jnp.zeros((1, LANES), F32)
    carry = block_scan((z, z, z, z), ncl, ncl, n_ctx_chunks)
    nblk = ncl // SUBLANES

    def step(b, carry):
        rf0 = pl.multiple_of(b * SUBLANES, SUBLANES)
        rb0 = pl.multiple_of((nblk - 1 - b) * SUBLANES, SUBLANES)
        return block_scan(carry, rf0, rb0, SUBLANES)

    lax.fori_loop(0, nblk, step, carry)

    y = _dot(ub, t_ref[...]) + _dot(sin_ref[0:nc, :].astype(BF16), w3_ref[...])
    y_ref[...] = y + d_ref[...] * u


def _ssm(s_ctx, s_lat, a_re, a_im, log_dt, b_re, b_im, c_re, c_im, d_skip):
    G, H, P, L, LH = SSM_GROUPS, SSM_GROUP, SSM_STATE, SSM_CHUNK, SSM_LH
    n_ctx, n_lat = s_ctx.shape[0], s_lat.shape[0]
    assert n_ctx % L == 0 and n_ctx // L <= SUBLANES and n_lat % (L * SUBLANES) == 0
    nc = (n_ctx + n_lat) // L
    u = jnp.concatenate([s_lat, s_ctx], axis=0)
    ur = u.reshape(nc, L, G, H).transpose(2, 0, 1, 3).reshape(G, nc, LH)
    ldt = jnp.broadcast_to(log_dt[:, :, None], (2, G, P))
    rows = jnp.stack([a_re[0], a_im[0], ldt[0], a_re[1], a_im[1], ldt[1],
                      jnp.zeros_like(a_re[0]), jnp.zeros_like(a_re[0])], axis=1)
    cols = rows.transpose(0, 2, 1)
    b_col = jnp.stack([b_re[0], b_im[0], b_re[1], b_im[1]], axis=1)
    b_t = b_col.transpose(0, 1, 3, 2)
    c_t = jnp.stack([c_re[0], c_im[0], c_re[1], c_im[1]], axis=1).transpose(0, 1, 3, 2)
    d_row = jnp.tile(d_skip.reshape(G, 1, H), (1, 1, L))

    def gspec(shape):
        nd = len(shape)
        return pl.BlockSpec((None,) + shape, lambda g: (g,) + (0,) * nd)

    yr = pl.pallas_call(
        functools.partial(_ssm_kernel, n_ctx_chunks=n_ctx // L),
        grid=(G,),
        in_specs=[gspec((nc, LH)), gspec((P, 8)), gspec((8, P)), gspec((4, P, H)), gspec((4, H, P)),
                  gspec((4, P, H)), gspec((1, LH))],
        out_specs=gspec((nc, LH)),
        out_shape=jax.ShapeDtypeStruct((G, nc, LH), F32),
        scratch_shapes=[
            pltpu.VMEM((LH, LH), BF16),
            pltpu.VMEM((8 * P, LH), BF16),
            pltpu.VMEM((8 * P, LH), BF16),
            pltpu.VMEM((n_lat // L + SUBLANES, 4 * LANES), F32),
            pltpu.VMEM((n_lat // L + SUBLANES, 4 * LANES), F32),
        ],
        compiler_params=_params("arbitrary"),
        name="ssm",
    )(ur, cols, rows, b_col, b_t, c_t, d_row)
    y = yr.reshape(G, nc, L, H).transpose(1, 2, 0, 3).reshape(nc * L, G * H)
    return y[n_lat:], y[:n_lat]


def _attn_kernel(q_ref, k_ref, v_ref, o_ref, m_ref, l_ref, acc_ref):
    nch = k_ref.shape[0]
    m_ref[...] = jnp.full_like(m_ref, -jnp.inf)
    l_ref[...] = jnp.zeros_like(l_ref)
    acc_ref[...] = jnp.zeros_like(acc_ref)

    def body(c, carry):
        kc = k_ref[c]
        vc = v_ref[c]
        for h in range(KV_REP):
            s = _dot(kc, q_ref[h])
            m_old = m_ref[h]
            m_new = jnp.maximum(m_old, jnp.max(s, axis=0, keepdims=True))
            alpha = jnp.exp2(m_old - m_new)
            p = jnp.exp2(s - m_new)
            l_ref[h] = alpha * l_ref[h] + jnp.sum(p, axis=0, keepdims=True)
            acc_ref[h] = alpha * acc_ref[h] + _dot(vc, p.astype(BF16))
            m_ref[h] = m_new
        return carry

    lax.fori_loop(0, nch, body, 0)
    for h in range(KV_REP):
        o_ref[h] = (acc_ref[h] / l_ref[h]).astype(BF16)


def _attn_tiles(nq, nk):
    tq = 512 if nq % 512 == 0 else 256
    for tk in (1280, 1024, 512, 256):
        if nk % tk == 0:
            return tq, tk
    raise ValueError(f"key count {nk} must be a multiple of 256")


def _attention(q, k, v):
    nq, nk = q.shape[0], k.shape[0]
    tq, tk = _attn_tiles(nq, nk)
    nch = nk // tk
    qt = q.reshape(nq, N_KV_HEADS, KV_REP, HEAD_DIM).transpose(1, 2, 3, 0)
    kr = k.reshape(nch, tk, N_KV_HEADS, HEAD_DIM).transpose(2, 0, 1, 3)
    vt = v.reshape(nch, tk, N_KV_HEADS, HEAD_DIM).transpose(2, 0, 3, 1)
    ot = pl.pallas_call(
        _attn_kernel,
        grid=(N_KV_HEADS, nq // tq),
        in_specs=[
            pl.BlockSpec((None, KV_REP, HEAD_DIM, tq), lambda j, i: (j, 0, 0, i)),
            pl.BlockSpec((None, nch, tk, HEAD_DIM), lambda j, i: (j, 0, 0, 0)),
            pl.BlockSpec((None, nch, HEAD_DIM, tk), lambda j, i: (j, 0, 0, 0)),
        ],
        out_specs=pl.BlockSpec((None, KV_REP, HEAD_DIM, tq), lambda j, i: (j, 0, 0, i)),
        out_shape=jax.ShapeDtypeStruct((N_KV_HEADS, KV_REP, HEAD_DIM, nq), BF16),
        scratch_shapes=[
            pltpu.VMEM((KV_REP, 1, tq), F32),
            pltpu.VMEM((KV_REP, 1, tq), F32),
            pltpu.VMEM((KV_REP, HEAD_DIM, tq), F32),
        ],
        compiler_params=_params("arbitrary", "arbitrary"),
        name="attention",
    )(qt, kr, vt)
    return ot.transpose(3, 0, 1, 2).reshape(nq, Q_W)


def _merge_kernel(x_ref, a_ref, ap_ref, an_ref, y_ref, o_ref, mod_ref, g_ref, wg_ref, pw_ref, ps_ref,
                  po_ref, wglu_ref, wao_ref, wout_ref, out_ref, ext_ref, *, n_rows):
    tm = x_ref.shape[0]
    i = pl.program_id(0)
    x = x_ref[...]
    h = _modulate(x, g_ref[...], mod_ref[0:1, :], mod_ref[1:2, :]).astype(BF16)
    gates = jax.nn.sigmoid(_dot(h, wg_ref[...]))

    a = a_ref[...]
    ext_ref[0:POOL_HALO, :] = jnp.where(i > 0, ap_ref[...], 0.0)
    ext_ref[POOL_HALO:POOL_HALO + tm, :] = a
    ext_ref[POOL_HALO + tm:, :] = jnp.where(i < pl.num_programs(0) - 1, an_ref[...], 0.0)
    e = ext_ref[...]
    r = tm + 2 * POOL_HALO
    s2 = e + pltpu.roll(e, 1, 0)
    s4 = pltpu.roll(s2, 1, 0) + pltpu.roll(s2, r - 1, 0)
    s8 = pltpu.roll(s4, 2, 0) + pltpu.roll(s4, r - 2, 0)
    s16 = pltpu.roll(s8, 4, 0) + pltpu.roll(s8, r - 4, 0)
    lane_grp = lax.broadcasted_iota(jnp.int32, (1, POOL_W), 1) // POOL_GROUP
    sums = jnp.where(lane_grp == 0, s2, jnp.where(lane_grp == 1, s4, jnp.where(lane_grp == 2, s8, s16)))
    sums = sums[POOL_HALO:POOL_HALO + tm, :]
    half_w = jnp.left_shift(1, lane_grp)
    row = i * tm + lax.broadcasted_iota(jnp.int32, (tm, 1), 0)
    cnt = jnp.minimum(row + half_w, n_rows) - jnp.maximum(row - half_w, 0)
    pooled = sums / cnt.astype(F32) - a
    ya = _dot(pooled.astype(BF16), pw_ref[...]) * ps_ref[...]
    br_a = _dot(ya.astype(BF16), po_ref[...])

    z = _dot(jax.nn.gelu(y_ref[...]).astype(BF16), wglu_ref[...])
    br_b = z[:, :D_MODEL] * jax.nn.sigmoid(z[:, D_MODEL:])

    br_c = _dot(o_ref[...], wao_ref[...])

    mixed = (gates[:, :D_MODEL] * br_a + gates[:, D_MODEL:2 * D_MODEL] * br_b
             + gates[:, 2 * D_MODEL:] * br_c)
    out_ref[...] = x + mod_ref[2:3, :] * _dot(mixed.astype(BF16), wout_ref[...])


def _merge(x, a, y_ssm, o_attn, mod3, g, w_g, pool_bd, pool_scale, pool_out, w_glu, w_ao, w_out):
    n, d = x.shape
    tm = _row_tile(n)
    nb = tm // POOL_HALO
    last = n // POOL_HALO - 1
    row = lambda w: pl.BlockSpec((tm, w), lambda i: (i, 0))
    return pl.pallas_call(
        functools.partial(_merge_kernel, n_rows=n),
        grid=(n // tm,),
        in_specs=[
            row(d), row(POOL_W),
            pl.BlockSpec((POOL_HALO, POOL_W), lambda i: (jnp.maximum(i * nb - 1, 0), 0)),
            pl.BlockSpec((POOL_HALO, POOL_W), lambda i: (jnp.minimum((i + 1) * nb, last), 0)),
            row(SSM_W), row(Q_W),
            _const_spec((3, d)), _const_spec((1, d)),
            _const_spec(w_g.shape), _const_spec(pool_bd.shape), _const_spec((1, POOL_W)),
            _const_spec(pool_out.shape), _const_spec(w_glu.shape), _const_spec(w_ao.shape),
            _const_spec(w_out.shape),
        ],
        out_specs=row(d),
        out_shape=jax.ShapeDtypeStruct((n, d), F32),
        scratch_shapes=[pltpu.VMEM((tm + 2 * POOL_HALO, POOL_W), F32)],
        compiler_params=_params("arbitrary"),
        name="merge",
    )(x, a, a, a, y_ssm, o_attn, mod3, g.reshape(1, d), w_g, pool_bd, pool_scale.reshape(1, POOL_W),
      pool_out, w_glu, w_ao, w_out)


def _block_diag(blocks):
    g, r, c = blocks.shape
    out = jnp.zeros((g * r, g * c), blocks.dtype)
    for i in range(g):
        out = out.at[i * r:(i + 1) * r, i * c:(i + 1) * c].set(blocks[i])
    return out


def _interleave_w13(w13):
    d = w13.shape[0]
    nch = D_FF // FFN_CHUNK
    w = w13.reshape(d, 2, nch, FFN_CHUNK).transpose(0, 2, 1, 3)
    return w.reshape(d, 2 * D_FF).astype(BF16)


def kernel(x, c, ctx, c_ctx, mod_w, mod_b, norm_g, ffn_w13, ffn_w2, w_in, pool_w, pool_scale, pool_out, ssm_a_re, ssm_a_im, ssm_log_dt, ssm_b_re, ssm_b_im, ssm_c_re, ssm_c_im, ssm_d, ssm_glu_w, q_norm_g, k_norm_g, attn_out, w_out, final_norm_g):
    batch, n, d = x.shape
    n_ctx = ctx.shape[1]
    depth = mod_w.shape[0]
    assert batch == 1 and d == D_MODEL and n % GRID_W == 0
    xl = x[0]
    xc = ctx[0]

    cond8 = jnp.zeros((SUBLANES, d), F32).at[0].set(c[0]).at[1].set(c_ctx)
    mods = _adaln(cond8, mod_w, mod_b).reshape(depth, SUBLANES, N_MOD, d)

    rope_tabs = _rope_tables(n)
    seg = _block_diag(jnp.ones((QK_W // HEAD_DIM, HEAD_DIM, HEAD_DIM), BF16))

    for l in range(depth):
        with_ctx_out = l < depth - 1
        m_lat, m_ctx = mods[l, 0], mods[l, 1]
        w13 = [_interleave_w13(ffn_w13[l, i]) for i in range(2)]
        w2 = [ffn_w2[l, i].astype(BF16) for i in range(2)]
        w_p = w_in[l][:, :N_PROJ].astype(BF16)
        w_g = w_in[l][:, N_PROJ:].astype(BF16)
        gv = jnp.concatenate([jnp.tile(q_norm_g[l], N_Q_HEADS) * Q_SCALE,
                              jnp.tile(k_norm_g[l], N_KV_HEADS)]).reshape(1, QK_W)
        merge_w = (w_g, _block_diag(pool_w[l]).astype(BF16), pool_scale[l], pool_out[l].astype(BF16),
                   ssm_glu_w[l].astype(BF16), attn_out[l].astype(BF16), w_out[l].astype(BF16))

        xl = _ffn(xl, m_lat[0:3], norm_g[l, 0], w13[0], w2[0])
        xc = _ffn(xc, m_ctx[0:3], norm_g[l, 0], w13[0], w2[0])
        a_l, s_l, q_l, k_l, v_l = _inproj(xl, m_lat[3:5], norm_g[l, 1], w_p, seg, gv, rope_tabs)
        a_c, s_c, q_c, k_c, v_c = _inproj(xc, m_ctx[3:5], norm_g[l, 1], w_p, seg, gv)
        y_c, y_l = _ssm(s_c, s_l, ssm_a_re[l], ssm_a_im[l], ssm_log_dt[l], ssm_b_re[l], ssm_b_im[l],
                        ssm_c_re[l], ssm_c_im[l], ssm_d[l])
        o_l = _attention(q_l, jnp.concatenate([k_l, k_c], axis=0), jnp.concatenate([v_l, v_c], axis=0))
        xl = _merge(xl, a_l, y_l, o_l, m_lat[3:6], norm_g[l, 1], *merge_w)
        if with_ctx_out:
            o_c = _attention(q_c, k_c, v_c)
            xc = _merge(xc, a_c, y_c, o_c, m_ctx[3:6], norm_g[l, 1], *merge_w)
        last = l == depth - 1
        xl = _ffn(xl, m_lat[6:9], norm_g[l, 2], w13[1], w2[1], final_norm_g if last else None)
        if with_ctx_out:
            xc = _ffn(xc, m_ctx[6:9], norm_g[l, 2], w13[1], w2[1])
    return xl[None]
```

```python
import functools
import math

import jax
import jax.numpy as jnp
from jax import lax
from jax.experimental import pallas as pl
from jax.experimental.pallas import tpu as pltpu

F32 = jnp.float32
BF16 = jnp.bfloat16

D_MODEL = 1024
N_MOD = 9
D_FF = 2816
GRID_W = 64
POOL_GROUP = 64
POOL_W = 256
SSM_W = 256
SSM_GROUP = 16
SSM_GROUPS = 16
SSM_STATE = 64
HEAD_DIM = 64
N_Q_HEADS = 8
N_KV_HEADS = 2
KV_REP = N_Q_HEADS // N_KV_HEADS
Q_W = N_Q_HEADS * HEAD_DIM
KV_W = N_KV_HEADS * HEAD_DIM
ROPE_THETA = 10000.0
EPS = 1e-6
N_PROJ = POOL_W + SSM_W + Q_W + 2 * KV_W
QK_W = Q_W + KV_W

LANES = 128
SUBLANES = 8
VMEM_LIMIT_BYTES = 56 * 1024 * 1024

SSM_CHUNK = 64
SSM_LH = SSM_CHUNK * SSM_GROUP
POOL_HALO = 8
Q_SCALE = HEAD_DIM ** -0.5 * math.log2(math.e)


def _row_tile(n):
    for t in (512, 256):
        if n % t == 0:
            return t
    raise ValueError(f"row count {n} must be a multiple of 256")


def _const_spec(shape):
    nd = len(shape)
    return pl.BlockSpec(shape, lambda *_: (0,) * nd, pipeline_mode=pl.Buffered(1))


def _params(*sem):
    return pltpu.CompilerParams(dimension_semantics=sem, vmem_limit_bytes=VMEM_LIMIT_BYTES)


def _modulate(x, g, shift, scale):
    ms = jnp.mean(x * x, axis=-1, keepdims=True)
    xn = x * lax.rsqrt(ms + EPS) * g
    return xn * (1.0 + scale) + shift


def _split2(x):
    hi = x.astype(BF16)
    lo = (x - hi.astype(F32)).astype(BF16)
    return hi, lo


def _dot(a, b):
    return jnp.dot(a, b, preferred_element_type=F32)


def _dot_sel(x, sel):
    x0 = x.astype(BF16)
    r1 = x - x0.astype(F32)
    x1 = r1.astype(BF16)
    x2 = (r1 - x1.astype(F32)).astype(BF16)
    return _dot(x0, sel) + _dot(x1, sel) + _dot(x2, sel)


def _dot_f32(a, b):
    ah, al = _split2(a)
    bh, bl = _split2(b)
    return _dot(ah, bh) + _dot(ah, bl) + _dot(al, bh)


def _adaln_kernel(c_ref, w_ref, b_ref, o_ref):
    c = c_ref[...]
    cs = c * jax.nn.sigmoid(c)
    o_ref[...] = _dot_f32(cs, w_ref[...]) + b_ref[...]


def _adaln(cond8, mod_w, mod_b):
    depth, d, nm = mod_w.shape
    tn = 1024
    return pl.pallas_call(
        _adaln_kernel,
        grid=(depth, nm // tn),
        in_specs=[
            pl.BlockSpec((SUBLANES, d), lambda l, j: (0, 0)),
            pl.BlockSpec((None, d, tn), lambda l, j: (l, 0, j)),
            pl.BlockSpec((None, 1, tn), lambda l, j: (l, 0, j)),
        ],
        out_specs=pl.BlockSpec((None, SUBLANES, tn), lambda l, j: (l, 0, j)),
        out_shape=jax.ShapeDtypeStruct((depth, SUBLANES, nm), F32),
        compiler_params=_params("arbitrary", "arbitrary"),
        name="adaln",
    )(cond8, mod_w, mod_b.reshape(depth, 1, nm))


def _ffn_kernel(x_ref, mod_ref, g_ref, w13_ref, w2_ref, *rest, final):
    o_ref = rest[-1]
    x = x_ref[...]
    h = _modulate(x, g_ref[...], mod_ref[0:1, :], mod_ref[1:2, :]).astype(BF16)
    g = _dot(h, w13_ref[:, :D_FF])
    u = _dot(h, w13_ref[:, D_FF:])
    acc = _dot((g * jax.nn.sigmoid(g) * u).astype(BF16), w2_ref[...])
    y = x + (0.5 * mod_ref[2:3, :]) * acc
    if final:
        fg_ref = rest[0]
        y = y * lax.rsqrt(jnp.mean(y * y, axis=-1, keepdims=True) + EPS) * fg_ref[...]
    o_ref[...] = y


def _ffn(x, mod3, g, w13, w2, final_g=None):
    n, d = x.shape
    tm = _row_tile(n)
    final = final_g is not None
    in_specs = [
        pl.BlockSpec((tm, d), lambda i: (i, 0)),
        _const_spec((3, d)),
        _const_spec((1, d)),
        _const_spec(w13.shape),
        _const_spec(w2.shape),
    ]
    args = [x, mod3, g.reshape(1, d), w13, w2]
    if final:
        in_specs.append(_const_spec((1, d)))
        args.append(final_g.reshape(1, d))
    return pl.pallas_call(
        functools.partial(_ffn_kernel, final=final),
        grid=(n // tm,),
        in_specs=in_specs,
        out_specs=pl.BlockSpec((tm, d), lambda i: (i, 0)),
        out_shape=jax.ShapeDtypeStruct((n, d), F32),
        compiler_params=_params("arbitrary"),
        name="ffn_final" if final else "ffn",
    )(*args)


ROT = HEAD_DIM // 4


def _inproj_kernel(x_ref, mod_ref, g_ref, ws_ref, wt_ref, gn_ref, *rest, rope):
    if rope:
        rope_ref = rest[0]
        rest = rest[1:]
    a_ref, s_ref, q_ref, k_ref, v_ref = rest
    tm = x_ref.shape[0]
    x = x_ref[...]
    h = _modulate(x, g_ref[...], mod_ref[0:1, :], mod_ref[1:2, :]).astype(BF16)
    ps = _dot(h, ws_ref[...])
    a_ref[...] = ps[:, 0:POOL_W]
    s_ref[...] = ps[:, POOL_W:]
    pt = lax.dot_general(wt_ref[...], h, (((1,), (1,)), ((), ())), preferred_element_type=F32)
    k_heads = []
    for hd in range(N_Q_HEADS + N_KV_HEADS):
        blk = pt[hd * HEAD_DIM:(hd + 1) * HEAD_DIM, :]
        ss = jnp.sum(blk * blk, axis=0, keepdims=True)
        xn = blk * lax.rsqrt(ss * (1.0 / HEAD_DIM) + EPS) * gn_ref[0 if hd < N_Q_HEADS else 1]
        if rope:
            x0, x1, x2, x3 = (xn[i * ROT:(i + 1) * ROT, :] for i in range(4))
            ca, sa, cb, sb = (rope_ref[i * ROT:(i + 1) * ROT, :] for i in range(4))
            xn = jnp.concatenate([x0 * ca - x1 * sa, x1 * ca + x0 * sa, x2 * cb - x3 * sb, x3 * cb + x2 * sb], axis=0)
        if hd < N_Q_HEADS:
            q_ref[hd] = xn.astype(BF16)
        else:
            k_heads.append(xn)
    k_ref[...] = jnp.concatenate(k_heads, axis=0).T.astype(BF16)
    v_ref[...] = pt[QK_W:, :].reshape(N_KV_HEADS, HEAD_DIM, tm).astype(BF16)


def _inproj(x, mod2, g, w_s, w_t, gn, rope_tab=None):
    n, d = x.shape
    tm = _row_tile(n)
    rope = rope_tab is not None
    in_specs = [
        pl.BlockSpec((tm, d), lambda i: (i, 0)),
        _const_spec((2, d)),
        _const_spec((1, d)),
        _const_spec(w_s.shape),
        _const_spec(w_t.shape),
        _const_spec((2, HEAD_DIM, tm)),
    ]
    args = [x, mod2, g.reshape(1, d), w_s, w_t, jnp.broadcast_to(gn[:, :, None], (2, HEAD_DIM, tm))]
    if rope:
        in_specs.append(pl.BlockSpec((HEAD_DIM, tm), lambda i: (0, i)))
        args.append(rope_tab)
    return pl.pallas_call(
        functools.partial(_inproj_kernel, rope=rope),
        grid=(n // tm,),
        in_specs=in_specs,
        out_specs=[
            pl.BlockSpec((tm, POOL_W), lambda i: (i, 0)),
            pl.BlockSpec((tm, SSM_W), lambda i: (i, 0)),
            pl.BlockSpec((N_Q_HEADS, HEAD_DIM, tm), lambda i: (0, 0, i)),
            pl.BlockSpec((tm, KV_W), lambda i: (i, 0)),
            pl.BlockSpec((N_KV_HEADS, HEAD_DIM, tm), lambda i: (0, 0, i)),
        ],
        out_shape=[
            jax.ShapeDtypeStruct((n, POOL_W), F32),
            jax.ShapeDtypeStruct((n, SSM_W), F32),
            jax.ShapeDtypeStruct((N_Q_HEADS, HEAD_DIM, n), BF16),
            jax.ShapeDtypeStruct((n, KV_W), BF16),
            jax.ShapeDtypeStruct((N_KV_HEADS, HEAD_DIM, n), BF16),
        ],
        compiler_params=_params("arbitrary"),
        name="inproj_rope" if rope else "inproj",
    )(*args)


def _rope_table(n):
    t = jnp.arange(n)
    rows = (t // GRID_W).astype(F32)
    cols = (t % GRID_W).astype(F32)
    half = HEAD_DIM // 2
    inv = ROPE_THETA ** (-jnp.arange(0, half, 2, dtype=F32) / half)
    ang_r = inv[:, None] * rows[None, :]
    ang_c = inv[:, None] * cols[None, :]
    return jnp.concatenate([jnp.cos(ang_r), jnp.sin(ang_r), jnp.cos(ang_c), jnp.sin(ang_c)], axis=0)


def _ssm_kernel(u_ref, pc_ref, pr_ref, bc_ref, bt_ref, ct_ref, d_ref, y_ref,
                t_ref, w1_ref, w3_ref, sl_ref, sin_ref, *, n_ctx_chunks):
    L, H, P, LH = SSM_CHUNK, SSM_GROUP, SSM_STATE, SSM_LH
    nc = u_ref.shape[0]
    pc = pc_ref[...]
    pr = pr_ref[...]

    def powers(o):
        dt = jnp.exp(pc[:, o + 2:o + 3])
        zr = pc[:, o:o + 1] * dt
        zi = pc[:, o + 1:o + 2] * dt
        m = lax.broadcasted_iota(jnp.int32, (1, LANES), 1).astype(F32)
        mag = jnp.exp(zr * m)
        return mag * jnp.cos(zi * m), mag * jnp.sin(zi * m)

    def zoh(lr, li, ldt):
        dt = jnp.exp(ldt)
        mag = jnp.exp(lr * dt)
        nr = mag * jnp.cos(li * dt) - 1.0
        ni = mag * jnp.sin(li * dt)
        den = lr * lr + li * li
        return (nr * lr + ni * li) / den, (ni * lr - nr * li) / den

    def sel_mat(expo):
        w = expo.shape[1]
        return (lax.broadcasted_iota(jnp.int32, (LANES, w), 0) == expo).astype(BF16)

    def chan_sel(w):
        lane = lax.broadcasted_iota(jnp.int32, (H, w), 1)
        return (lane % H == lax.broadcasted_iota(jnp.int32, (H, w), 0)).astype(BF16)

    def cmul(ar, ai, br, bi):
        return ar * br - ai * bi, ar * bi + ai * br

    tok_lh = lax.broadcasted_iota(jnp.int32, (1, LH), 1) // H
    lag2 = lax.broadcasted_iota(jnp.int32, (1, 2 * LH), 1) // H - (L - 1)
    sel16_lh = chan_sel(LH)
    sel16_2lh = chan_sel(2 * LH)

    w1_ref[...] = jnp.zeros_like(w1_ref)
    w3_ref[...] = jnp.zeros_like(w3_ref)
    klong = None
    a_chunk = []
    for dr in range(2):
        o = 3 * dr
        apr, api = powers(o)
        qr, qi = zoh(pr[o:o + 1, :], pr[o + 1:o + 2, :], pr[o + 2:o + 3, :])
        bbr_t, bbi_t = cmul(qr, qi, bt_ref[2 * dr], bt_ref[2 * dr + 1])
        lag = lag2 if dr == 0 else -lag2
        sel = sel_mat(lag)
        er, ei = _dot_sel(apr, sel), _dot_sel(api, sel)
        cr = _dot_sel(ct_ref[2 * dr], sel16_2lh)
        ci = _dot_sel(ct_ref[2 * dr + 1], sel16_2lh)
        gr, gi = cmul(er, ei, cr, ci)
        kd = _dot_f32(bbr_t, gr) - _dot_f32(bbi_t, gi)
        klong = kd if klong is None else klong + kd
        qrc, qic = zoh(pc[:, o:o + 1], pc[:, o + 1:o + 2], pc[:, o + 2:o + 3])
        bbr, bbi = cmul(qrc, qic, bc_ref[2 * dr], bc_ref[2 * dr + 1])
        sel = sel_mat((L - 1 - tok_lh) if dr == 0 else tok_lh)
        er, ei = _dot_sel(apr, sel), _dot_sel(api, sel)
        wr, wi = cmul(er, ei, _dot_sel(bbr, sel16_lh), _dot_sel(bbi, sel16_lh))
        w1_ref[(4 * dr) * P:(4 * dr + 1) * P, :] = wr.astype(BF16)
        w1_ref[(4 * dr + 2) * P:(4 * dr + 3) * P, :] = wi.astype(BF16)
        sel = sel_mat((tok_lh + 1) if dr == 0 else (L - tok_lh))
        er, ei = _dot_sel(apr, sel), _dot_sel(api, sel)
        gr, gi = cmul(er, ei, _dot_sel(ct_ref[2 * dr], sel16_lh), _dot_sel(ct_ref[2 * dr + 1], sel16_lh))
        w3_ref[(4 * dr) * P:(4 * dr + 1) * P, :] = gr.astype(BF16)
        w3_ref[(4 * dr + 2) * P:(4 * dr + 3) * P, :] = (-gi).astype(BF16)
        dtr = jnp.exp(pr[o + 2:o + 3, :])
        mag = jnp.exp(pr[o:o + 1, :] * dtr * L)
        ang = pr[o + 1:o + 2, :] * dtr * L
        pad = jnp.zeros((1, LANES - P), F32)
        a_chunk.append((jnp.concatenate([mag * jnp.cos(ang), pad], axis=1),
                        jnp.concatenate([mag * jnp.sin(ang), pad], axis=1)))

    for j in range(L):
        start = (L - 1 - j) * H
        t_ref[j * H:(j + 1) * H, :] = klong[:, start:start + LH].astype(BF16)

    u = u_ref[...]
    ub = u.astype(BF16)
    ncl = nc - n_ctx_chunks
    sl_ref[ncl:, :] = jnp.zeros((SUBLANES, 4 * LANES), F32)
    sl_ref[0:nc, :] = lax.dot_general(ub, w1_ref[...], (((1,), (1,)), ((), ())), preferred_element_type=F32)

    (afr, afi), (abr, abi) = a_chunk
    rid = lax.broadcasted_iota(jnp.int32, (SUBLANES, LANES), 0)

    def block_scan(carry, rf0, rb0, n_rows):
        fr, fi, br, bi = carry
        lf_r, lf_i = sl_ref[pl.ds(rf0, SUBLANES), 0:LANES], sl_ref[pl.ds(rf0, SUBLANES), LANES:2 * LANES]
        lb_r = sl_ref[pl.ds(rb0, SUBLANES), 2 * LANES:3 * LANES]
        lb_i = sl_ref[pl.ds(rb0, SUBLANES), 3 * LANES:4 * LANES]
        of_r = of_i = ob_r = ob_i = jnp.zeros((SUBLANES, LANES), F32)
        for k in range(n_rows):
            r = k
            of_r = jnp.where(rid == r, fr, of_r)
            of_i = jnp.where(rid == r, fi, of_i)
            fr, fi = (afr * fr - afi * fi + lf_r[r:r + 1, :], afr * fi + afi * fr + lf_i[r:r + 1, :])
            r = n_rows - 1 - k
            ob_r = jnp.where(rid == r, br, ob_r)
            ob_i = jnp.where(rid == r, bi, ob_i)
            br, bi = (abr * br - abi * bi + lb_r[r:r + 1, :], abr * bi + abi * br + lb_i[r:r + 1, :])
        sin_ref[pl.ds(rf0, SUBLANES), 0:LANES] = of_r
        sin_ref[pl.ds(rf0, SUBLANES), LANES:2 * LANES] = of_i
        sin_ref[pl.ds(rb0, SUBLANES), 2 * LANES:3 * LANES] = ob_r
        sin_ref[pl.ds(rb0, SUBLANES), 3 * LANES:4 * LANES] = ob_i
        return fr, fi, br, bi

    z = jnp.zeros((1, LANES), F32)
    carry = block_scan((z, z, z, z), ncl, ncl, n_ctx_chunks)
    nblk = ncl // SUBLANES

    def step(b, carry):
        rf0 = pl.multiple_of(b * SUBLANES, SUBLANES)
        rb0 = pl.multiple_of((nblk - 1 - b) * SUBLANES, SUBLANES)
        return block_scan(carry, rf0, rb0, SUBLANES)

    lax.fori_loop(0, nblk, step, carry)

    y = _dot(ub, t_ref[...]) + _dot(sin_ref[0:nc, :].astype(BF16), w3_ref[...])
    y_ref[...] = y + d_ref[...] * u


def _ssm(s_ctx, s_lat, a_re, a_im, log_dt, b_re, b_im, c_re, c_im, d_skip):
    G, H, P, L, LH = SSM_GROUPS, SSM_GROUP, SSM_STATE, SSM_CHUNK, SSM_LH
    n_ctx, n_lat = s_ctx.shape[0], s_lat.shape[0]
    assert n_ctx % L == 0 and n_ctx // L <= SUBLANES and n_lat % (L * SUBLANES) == 0
    nc = (n_ctx + n_lat) // L
    u = jnp.concatenate([s_lat, s_ctx], axis=0)
    ur = u.reshape(nc, L, G, H).transpose(2, 0, 1, 3).reshape(G, nc, LH)
    ldt = jnp.broadcast_to(log_dt[:, :, None], (2, G, P))
    rows = jnp.stack([a_re[0], a_im[0], ldt[0], a_re[1], a_im[1], ldt[1],
                      jnp.zeros_like(a_re[0]), jnp.zeros_like(a_re[0])], axis=1)
    cols = rows.transpose(0, 2, 1)
    b_col = jnp.stack([b_re[0], b_im[0], b_re[1], b_im[1]], axis=1)
    b_t = b_col.transpose(0, 1, 3, 2)
    c_t = jnp.stack([c_re[0], c_im[0], c_re[1], c_im[1]], axis=1).transpose(0, 1, 3, 2)
    d_row = jnp.tile(d_skip.reshape(G, 1, H), (1, 1, L))

    def gspec(shape):
        nd = len(shape)
        return pl.BlockSpec((None,) + shape, lambda g: (g,) + (0,) * nd)

    yr = pl.pallas_call(
        functools.partial(_ssm_kernel, n_ctx_chunks=n_ctx // L),
        grid=(G,),
        in_specs=[gspec((nc, LH)), gspec((P, 8)), gspec((8, P)), gspec((4, P, H)), gspec((4, H, P)),
                  gspec((4, P, H)), gspec((1, LH))],
        out_specs=gspec((nc, LH)),
        out_shape=jax.ShapeDtypeStruct((G, nc, LH), F32),
        scratch_shapes=[
            pltpu.VMEM((LH, LH), BF16),
            pltpu.VMEM((8 * P, LH), BF16),
            pltpu.VMEM((8 * P, LH), BF16),
            pltpu.VMEM((n_lat // L + SUBLANES, 4 * LANES), F32),
            pltpu.VMEM((n_lat // L + SUBLANES, 4 * LANES), F32),
        ],
        compiler_params=_params("arbitrary"),
        name="ssm",
    )(ur, cols, rows, b_col, b_t, c_t, d_row)
    y = yr.reshape(G, nc, L, H).transpose(1, 2, 0, 3).reshape(nc * L, G * H)
    return y[n_lat:], y[:n_lat]


def _attn_kernel(q_ref, k_ref, v_ref, o_ref, s_ref, m_ref, l_ref, acc_ref):
    nch = k_ref.shape[0]
    m_ref[...] = jnp.full_like(m_ref, -jnp.inf)
    l_ref[...] = jnp.zeros_like(l_ref)
    acc_ref[...] = jnp.zeros_like(acc_ref)

    def scores(c, slot, h):
        s = _dot(k_ref[c], q_ref[h])
        s_ref[slot, h] = s
        return jnp.max(s, axis=0, keepdims=True)

    def consume(c, slot, h, mx):
        m_old = m_ref[h]
        m_new = jnp.maximum(m_old, mx)
        alpha = jnp.exp2(m_old - m_new)
        p = jnp.exp2(s_ref[slot, h] - m_new)
        l_ref[h] = alpha * l_ref[h] + jnp.sum(p, axis=0, keepdims=True)
        acc_ref[h] = alpha * acc_ref[h] + _dot(v_ref[c], p.astype(BF16))
        m_ref[h] = m_new

    def half(c, slot, mx):
        new = []
        for h in range(KV_REP):
            new.append(scores(c + 1, 1 - slot, h))
            consume(c, slot, h, mx[h])
        return tuple(new)

    def body(d, mx):
        return half(2 * d + 1, 1, half(2 * d, 0, mx))

    mx = tuple(scores(0, 0, h) for h in range(KV_REP))
    npair = (nch - 1) // 2
    mx = lax.fori_loop(0, npair, body, mx)
    c = 2 * npair
    if c < nch - 1:
        mx = half(c, 0, mx)
        c += 1
    for h in range(KV_REP):
        consume(c, c % 2, h, mx[h])
        o_ref[h] = (acc_ref[h] / l_ref[h]).astype(BF16)


def _attn_tiles(nq, nk):
    tq = 512 if nq % 512 == 0 else 256
    for tk in (1280, 1024, 512, 256):
        if nk % tk == 0:
            return tq, tk
    raise ValueError(f"key count {nk} must be a multiple of 256")


def _attention(qt, k, vt):
    nq, nk = qt.shape[2], k.shape[0]
    tq, tk = _attn_tiles(nq, nk)
    nch = nk // tk
    qt = qt.reshape(N_KV_HEADS, KV_REP, HEAD_DIM, nq)
    kr = k.reshape(nch, tk, N_KV_HEADS, HEAD_DIM).transpose(2, 0, 1, 3)
    vt = vt.reshape(N_KV_HEADS, HEAD_DIM, nch, tk).transpose(0, 2, 1, 3)
    ot = pl.pallas_call(
        _attn_kernel,
        grid=(N_KV_HEADS, nq // tq),
        in_specs=[
            pl.BlockSpec((None, KV_REP, HEAD_DIM, tq), lambda j, i: (j, 0, 0, i)),
            pl.BlockSpec((None, nch, tk, HEAD_DIM), lambda j, i: (j, 0, 0, 0)),
            pl.BlockSpec((None, nch, HEAD_DIM, tk), lambda j, i: (j, 0, 0, 0)),
        ],
        out_specs=pl.BlockSpec((None, KV_REP, HEAD_DIM, tq), lambda j, i: (j, 0, 0, i)),
        out_shape=jax.ShapeDtypeStruct((N_KV_HEADS, KV_REP, HEAD_DIM, nq), BF16),
        scratch_shapes=[
            pltpu.VMEM((2, KV_REP, tk, tq), F32),
            pltpu.VMEM((KV_REP, 1, tq), F32),
            pltpu.VMEM((KV_REP, 1, tq), F32),
            pltpu.VMEM((KV_REP, HEAD_DIM, tq), F32),
        ],
        compiler_params=_params("arbitrary", "arbitrary"),
        name="attention",
    )(qt, kr, vt)
    return ot.reshape(Q_W, nq)


def _merge_kernel(x_ref, a_ref, ap_ref, an_ref, y_ref, o_ref, mod_ref, g_ref, wg_ref, pw_ref, ps_ref,
                  po_ref, wglu_ref, wao_ref, wout_ref, out_ref, ext_ref, *, n_rows):
    tm = x_ref.shape[0]
    i = pl.program_id(0)
    x = x_ref[...]
    h = _modulate(x, g_ref[...], mod_ref[0:1, :], mod_ref[1:2, :]).astype(BF16)
    gates = jax.nn.sigmoid(_dot(h, wg_ref[...]))

    a = a_ref[...]
    ext_ref[0:POOL_HALO, :] = jnp.where(i > 0, ap_ref[...], 0.0)
    ext_ref[POOL_HALO:POOL_HALO + tm, :] = a
    ext_ref[POOL_HALO + tm:, :] = jnp.where(i < pl.num_programs(0) - 1, an_ref[...], 0.0)
    e = ext_ref[...]
    r = tm + 2 * POOL_HALO
    s2 = e + pltpu.roll(e, 1, 0)
    s4 = pltpu.roll(s2, 1, 0) + pltpu.roll(s2, r - 1, 0)
    s8 = pltpu.roll(s4, 2, 0) + pltpu.roll(s4, r - 2, 0)
    s16 = pltpu.roll(s8, 4, 0) + pltpu.roll(s8, r - 4, 0)
    lane_grp = lax.broadcasted_iota(jnp.int32, (1, POOL_W), 1) // POOL_GROUP
    sums = jnp.where(lane_grp == 0, s2, jnp.where(lane_grp == 1, s4, jnp.where(lane_grp == 2, s8, s16)))
    sums = sums[POOL_HALO:POOL_HALO + tm, :]
    half_w = jnp.left_shift(1, lane_grp)
    row = i * tm + lax.broadcasted_iota(jnp.int32, (tm, 1), 0)
    cnt = jnp.minimum(row + half_w, n_rows) - jnp.maximum(row - half_w, 0)
    pooled = sums / cnt.astype(F32) - a
    ya = _dot(pooled.astype(BF16), pw_ref[...]) * ps_ref[...]
    br_a = _dot(ya.astype(BF16), po_ref[...])

    z = _dot(jax.nn.gelu(y_ref[...]).astype(BF16), wglu_ref[...])
    br_b = z[:, :D_MODEL] * jax.nn.sigmoid(z[:, D_MODEL:])

    br_c = lax.dot_general(o_ref[...], wao_ref[...], (((0,), (0,)), ((), ())), preferred_element_type=F32)

    mixed = (gates[:, :D_MODEL] * br_a + gates[:, D_MODEL:2 * D_MODEL] * br_b
             + gates[:, 2 * D_MODEL:] * br_c)
    out_ref[...] = x + mod_ref[2:3, :] * _dot(mixed.astype(BF16), wout_ref[...])


def _merge(x, a, y_ssm, o_attn, mod3, g, w_g, pool_bd, pool_scale, pool_out, w_glu, w_ao, w_out):
    n, d = x.shape
    tm = _row_tile(n)
    nb = tm // POOL_HALO
    last = n // POOL_HALO - 1
    row = lambda w: pl.BlockSpec((tm, w), lambda i: (i, 0))
    return pl.pallas_call(
        functools.partial(_merge_kernel, n_rows=n),
        grid=(n // tm,),
        in_specs=[
            row(d), row(POOL_W),
            pl.BlockSpec((POOL_HALO, POOL_W), lambda i: (jnp.maximum(i * nb - 1, 0), 0)),
            pl.BlockSpec((POOL_HALO, POOL_W), lambda i: (jnp.minimum((i + 1) * nb, last), 0)),
            row(SSM_W), pl.BlockSpec((Q_W, tm), lambda i: (0, i)),
            _const_spec((3, d)), _const_spec((1, d)),
            _const_spec(w_g.shape), _const_spec(pool_bd.shape), _const_spec((1, POOL_W)),
            _const_spec(pool_out.shape), _const_spec(w_glu.shape), _const_spec(w_ao.shape),
            _const_spec(w_out.shape),
        ],
        out_specs=row(d),
        out_shape=jax.ShapeDtypeStruct((n, d), F32),
        scratch_shapes=[pltpu.VMEM((tm + 2 * POOL_HALO, POOL_W), F32)],
        compiler_params=_params("arbitrary"),
        name="merge",
    )(x, a, a, a, y_ssm, o_attn, mod3, g.reshape(1, d), w_g, pool_bd, pool_scale.reshape(1, POOL_W),
      pool_out, w_glu, w_ao, w_out)


def _block_diag(blocks):
    g, r, c = blocks.shape
    out = jnp.zeros((g * r, g * c), blocks.dtype)
    for i in range(g):
        out = out.at[i * r:(i + 1) * r, i * c:(i + 1) * c].set(blocks[i])
    return out


def kernel(x, c, ctx, c_ctx, mod_w, mod_b, norm_g, ffn_w13, ffn_w2, w_in, pool_w, pool_scale, pool_out, ssm_a_re, ssm_a_im, ssm_log_dt, ssm_b_re, ssm_b_im, ssm_c_re, ssm_c_im, ssm_d, ssm_glu_w, q_norm_g, k_norm_g, attn_out, w_out, final_norm_g):
    batch, n, d = x.shape
    n_ctx = ctx.shape[1]
    depth = mod_w.shape[0]
    assert batch == 1 and d == D_MODEL and n % GRID_W == 0
    xl = x[0]
    xc = ctx[0]

    cond8 = jnp.zeros((SUBLANES, d), F32).at[0].set(c[0]).at[1].set(c_ctx)
    mods = _adaln(cond8, mod_w, mod_b).reshape(depth, SUBLANES, N_MOD, d)

    rope_tab = _rope_table(n)

    for l in range(depth):
        with_ctx_out = l < depth - 1
        m_lat, m_ctx = mods[l, 0], mods[l, 1]
        w13 = [ffn_w13[l, i].astype(BF16) for i in range(2)]
        w2 = [ffn_w2[l, i].astype(BF16) for i in range(2)]
        w_s = w_in[l][:, :POOL_W + SSM_W].astype(BF16)
        w_t = w_in[l][:, POOL_W + SSM_W:N_PROJ].T.astype(BF16)
        w_g = w_in[l][:, N_PROJ:].astype(BF16)
        gn = jnp.stack([q_norm_g[l] * Q_SCALE, k_norm_g[l]])
        merge_w = (w_g, _block_diag(pool_w[l]).astype(BF16), pool_scale[l], pool_out[l].astype(BF16),
                   ssm_glu_w[l].astype(BF16), attn_out[l].astype(BF16), w_out[l].astype(BF16))

        xl = _ffn(xl, m_lat[0:3], norm_g[l, 0], w13[0], w2[0])
        xc = _ffn(xc, m_ctx[0:3], norm_g[l, 0], w13[0], w2[0])
        a_l, s_l, q_l, k_l, v_l = _inproj(xl, m_lat[3:5], norm_g[l, 1], w_s, w_t, gn, rope_tab)
        a_c, s_c, q_c, k_c, v_c = _inproj(xc, m_ctx[3:5], norm_g[l, 1], w_s, w_t, gn)
        y_c, y_l = _ssm(s_c, s_l, ssm_a_re[l], ssm_a_im[l], ssm_log_dt[l], ssm_b_re[l], ssm_b_im[l],
                        ssm_c_re[l], ssm_c_im[l], ssm_d[l])
        o_l = _attention(q_l, jnp.concatenate([k_l, k_c], axis=0), jnp.concatenate([v_l, v_c], axis=2))
        xl = _merge(xl, a_l, y_l, o_l, m_lat[3:6], norm_g[l, 1], *merge_w)
        if with_ctx_out:
            o_c = _attention(q_c, k_c, v_c)
            xc = _merge(xc, a_c, y_c, o_c, m_ctx[3:6], norm_g[l, 1], *merge_w)
        last = l == depth - 1
        xl = _ffn(xl, m_lat[6:9], norm_g[l, 2], w13[1], w2[1], final_norm_g if last else None)
        if with_ctx_out:
            xc = _ffn(xc, m_ctx[6:9], norm_g[l, 2], w13[1], w2[1])
    return xl[None]
```

```python
import functools
import math

import jax
import jax.numpy as jnp
from jax import lax
from jax.experimental import pallas as pl
from jax.experimental.pallas import tpu as pltpu

F32 = jnp.float32
BF16 = jnp.bfloat16

D_MODEL = 1024
N_MOD = 9
D_FF = 2816
GRID_W = 64
POOL_GROUP = 64
POOL_W = 256
SSM_W = 256
SSM_GROUP = 16
SSM_GROUPS = 16
SSM_STATE = 64
HEAD_DIM = 64
N_Q_HEADS = 8
N_KV_HEADS = 2
KV_REP = N_Q_HEADS // N_KV_HEADS
Q_W = N_Q_HEADS * HEAD_DIM
KV_W = N_KV_HEADS * HEAD_DIM
ROPE_THETA = 10000.0
EPS = 1e-6
N_PROJ = POOL_W + SSM_W + Q_W + 2 * KV_W
QK_W = Q_W + KV_W
ROT = HEAD_DIM // 4

LANES = 128
SUBLANES = 8
BF16_ROWS = 16
VMEM_LIMIT_BYTES = 56 * 1024 * 1024

ROW_TILE = 512
SSM_CHUNK = 64
SSM_LH = SSM_CHUNK * SSM_GROUP
POOL_HALO = 8
Q_SCALE = HEAD_DIM ** -0.5 * math.log2(math.e)


def _row_tile(n):
    for t in (ROW_TILE, ROW_TILE // 2):
        if n % t == 0:
            return t
    raise ValueError(f"row count {n} must be a multiple of {ROW_TILE // 2}")


def _pick(block, *index):
    return pl.BlockSpec(block, lambda *_: index, pipeline_mode=pl.Buffered(1))


def _params(*sem):
    return pltpu.CompilerParams(dimension_semantics=sem, vmem_limit_bytes=VMEM_LIMIT_BYTES)


def _modulate(x, g, shift, scale):
    ms = jnp.mean(x * x, axis=-1, keepdims=True)
    xn = x * lax.rsqrt(ms + EPS) * g
    return xn * (1.0 + scale) + shift


def _split2(x):
    hi = x.astype(BF16)
    lo = (x - hi.astype(F32)).astype(BF16)
    return hi, lo


def _dot(a, b):
    return jnp.dot(a, b, preferred_element_type=F32)


def _dot_sel(xs, sel):
    rows = xs[0].shape[0]
    parts = [part for x in xs for part in _split2(x)]
    out = _dot(jnp.concatenate(parts, axis=0), sel)
    return [out[2 * i * rows:(2 * i + 1) * rows] + out[(2 * i + 1) * rows:(2 * i + 2) * rows]
            for i in range(len(xs))]


def _dot_f32(a, b):
    ah, al = _split2(a)
    bh, bl = _split2(b)
    return _dot(ah, bh) + _dot(ah, bl) + _dot(al, bh)


def _adaln_kernel(c_ref, w_ref, b_ref, o_ref):
    c = c_ref[...]
    cs = c * jax.nn.sigmoid(c)
    o_ref[...] = _dot_f32(cs, w_ref[...]) + b_ref[...]


def _adaln(cond8, mod_w, mod_b):
    depth, d, nm = mod_w.shape
    tn = 1024
    return pl.pallas_call(
        _adaln_kernel,
        grid=(depth, nm // tn),
        in_specs=[
            pl.BlockSpec((SUBLANES, d), lambda l, j: (0, 0)),
            pl.BlockSpec((None, d, tn), lambda l, j: (l, 0, j)),
            pl.BlockSpec((None, 1, tn), lambda l, j: (l, 0, j)),
        ],
        out_specs=pl.BlockSpec((None, SUBLANES, tn), lambda l, j: (l, 0, j)),
        out_shape=jax.ShapeDtypeStruct((depth, SUBLANES, nm), F32),
        compiler_params=_params("arbitrary", "arbitrary"),
        name="adaln",
    )(cond8, mod_w, mod_b.reshape(depth, 1, nm))


def _ffn_kernel(x_ref, mod_ref, g_ref, w13_ref, w2_ref, *rest, norm_row, final):
    o_ref = rest[-1]
    x = x_ref[...]
    h = _modulate(x, g_ref[norm_row:norm_row + 1, :], mod_ref[0:1, :], mod_ref[1:2, :]).astype(BF16)
    g = _dot(h, w13_ref[:, :D_FF])
    u = _dot(h, w13_ref[:, D_FF:])
    acc = _dot((g * jax.nn.sigmoid(g) * u).astype(BF16), w2_ref[...])
    y = x + (0.5 * mod_ref[2:3, :]) * acc
    if final:
        fg_ref = rest[0]
        y = y * lax.rsqrt(jnp.mean(y * y, axis=-1, keepdims=True) + EPS) * fg_ref[...]
    o_ref[...] = y


def _ffn(x, p, l, stream, which, final_g=None):
    n, d = x.shape
    tm = _row_tile(n)
    final = final_g is not None
    in_specs = [
        pl.BlockSpec((tm, d), lambda i: (i, 0)),
        _pick((None, None, None, 3, d), l, stream, 2 * which, 0, 0),
        _pick((None, 3, d), l, 0, 0),
        _pick((None, None, d, 2 * D_FF), l, which, 0, 0),
        _pick((None, None, D_FF, d), l, which, 0, 0),
    ]
    args = [x, p["mods"], p["norm_g"], p["w13"], p["w2"]]
    if final:
        in_specs.append(_pick((1, d), 0, 0))
        args.append(final_g.reshape(1, d))
    return pl.pallas_call(
        functools.partial(_ffn_kernel, norm_row=2 * which, final=final),
        grid=(n // tm,),
        in_specs=in_specs,
        out_specs=pl.BlockSpec((tm, d), lambda i: (i, 0)),
        out_shape=jax.ShapeDtypeStruct((n, d), F32),
        compiler_params=_params("arbitrary"),
        name="ffn_final" if final else "ffn",
    )(*args)


def _inproj_kernel(x_ref, mod_ref, g_ref, ws_ref, wt_ref, gn_ref, *rest, rope):
    if rope:
        rope_ref = rest[0]
        rest = rest[1:]
    a_ref, s_ref, q_ref, k_ref, v_ref = rest
    tm = x_ref.shape[0]
    x = x_ref[...]
    h = _modulate(x, g_ref[1:2, :], mod_ref[0:1, :], mod_ref[1:2, :]).astype(BF16)
    ps = _dot(h, ws_ref[...])
    a_ref[...] = ps[:, 0:POOL_W]
    s_ref[...] = ps[:, POOL_W:]
    pt = lax.dot_general(wt_ref[...], h, (((1,), (1,)), ((), ())), preferred_element_type=F32)
    zeros = jnp.zeros((HEAD_DIM, tm), BF16)
    k_heads = []
    for hd in range(N_Q_HEADS + N_KV_HEADS):
        blk = pt[hd * HEAD_DIM:(hd + 1) * HEAD_DIM, :]
        ss = jnp.sum(blk * blk, axis=0, keepdims=True)
        xn = blk * lax.rsqrt(ss * (1.0 / HEAD_DIM) + EPS) * gn_ref[0 if hd < N_Q_HEADS else 1]
        if rope:
            x0, x1, x2, x3 = (xn[i * ROT:(i + 1) * ROT, :] for i in range(4))
            ca, sa, cb, sb = (rope_ref[i * ROT:(i + 1) * ROT, :] for i in range(4))
            xn = jnp.concatenate([x0 * ca - x1 * sa, x1 * ca + x0 * sa, x2 * cb - x3 * sb, x3 * cb + x2 * sb], axis=0)
        if hd < N_Q_HEADS:
            qb = xn.astype(BF16)
            q_ref[hd] = jnp.concatenate([qb, zeros] if hd < KV_REP else [zeros, qb], axis=0)
        else:
            k_heads.append(xn)
    k_ref[...] = jnp.concatenate(k_heads, axis=0).T.astype(BF16)
    v_ref[...] = pt[QK_W:, :].reshape(N_KV_HEADS, HEAD_DIM, tm).astype(BF16)


def _inproj(x, p, l, stream, rope_tab=None):
    n, d = x.shape
    tm = _row_tile(n)
    rope = rope_tab is not None
    in_specs = [
        pl.BlockSpec((tm, d), lambda i: (i, 0)),
        _pick((None, None, None, 3, d), l, stream, 1, 0, 0),
        _pick((None, 3, d), l, 0, 0),
        _pick((None, d, POOL_W + SSM_W), l, 0, 0),
        _pick((None, QK_W + KV_W, d), l, 0, 0),
        _pick((None, 2, HEAD_DIM, tm), l, 0, 0, 0),
    ]
    args = [x, p["mods"], p["norm_g"], p["w_s"], p["w_t"], p["gn"]]
    if rope:
        in_specs.append(pl.BlockSpec((HEAD_DIM, tm), lambda i: (0, i)))
        args.append(rope_tab)
    return pl.pallas_call(
        functools.partial(_inproj_kernel, rope=rope),
        grid=(n // tm,),
        in_specs=in_specs,
        out_specs=[
            pl.BlockSpec((tm, POOL_W), lambda i: (i, 0)),
            pl.BlockSpec((tm, SSM_W), lambda i: (i, 0)),
            pl.BlockSpec((None, N_Q_HEADS, KV_W, tm), lambda i: (i, 0, 0, 0)),
            pl.BlockSpec((tm, KV_W), lambda i: (i, 0)),
            pl.BlockSpec((N_KV_HEADS, HEAD_DIM, tm), lambda i: (0, 0, i)),
        ],
        out_shape=[
            jax.ShapeDtypeStruct((n, POOL_W), F32),
            jax.ShapeDtypeStruct((n, SSM_W), F32),
            jax.ShapeDtypeStruct((n // tm, N_Q_HEADS, KV_W, tm), BF16),
            jax.ShapeDtypeStruct((n, KV_W), BF16),
            jax.ShapeDtypeStruct((N_KV_HEADS, HEAD_DIM, n), BF16),
        ],
        compiler_params=_params("arbitrary"),
        name="inproj_rope" if rope else "inproj",
    )(*args)


def _rope_table(n):
    t = jnp.arange(n)
    rows = (t // GRID_W).astype(F32)
    cols = (t % GRID_W).astype(F32)
    half = HEAD_DIM // 2
    inv = ROPE_THETA ** (-jnp.arange(0, half, 2, dtype=F32) / half)
    ang_r = inv[:, None] * rows[None, :]
    ang_c = inv[:, None] * cols[None, :]
    return jnp.concatenate([jnp.cos(ang_r), jnp.sin(ang_r), jnp.cos(ang_c), jnp.sin(ang_c)], axis=0)


def _ssm_kernel(ul_ref, uc_ref, pc_ref, pr_ref, bc_ref, bt_ref, ct_ref, yl_ref, yc_ref,
                t_ref, w1_ref, w3_ref, sl_ref, sin_ref, *, n_ctx_chunks):
    L, H, P, LH = SSM_CHUNK, SSM_GROUP, SSM_STATE, SSM_LH
    ncl = ul_ref.shape[0]
    ctx_rows = uc_ref.shape[0]
    pc = pc_ref[...]
    pr = pr_ref[...]

    def powers(o):
        dt = jnp.exp(pc[:, o + 2:o + 3])
        zr = pc[:, o:o + 1] * dt
        zi = pc[:, o + 1:o + 2] * dt
        m = lax.broadcasted_iota(jnp.int32, (1, LANES), 1).astype(F32)
        mag = jnp.exp(zr * m)
        return mag * jnp.cos(zi * m), mag * jnp.sin(zi * m)

    def zoh(lr, li, ldt):
        dt = jnp.exp(ldt)
        mag = jnp.exp(lr * dt)
        nr = mag * jnp.cos(li * dt) - 1.0
        ni = mag * jnp.sin(li * dt)
        den = lr * lr + li * li
        return (nr * lr + ni * li) / den, (ni * lr - nr * li) / den

    def sel_mat(expo):
        w = expo.shape[1]
        return (lax.broadcasted_iota(jnp.int32, (LANES, w), 0) == expo).astype(BF16)

    def lane_tile(x, w):
        return jnp.concatenate([x] * (w // LANES), axis=1)

    def cmul(ar, ai, br, bi):
        return ar * br - ai * bi, ar * bi + ai * br

    tok_lh = lax.broadcasted_iota(jnp.int32, (1, LH), 1) // H
    lag2 = lax.broadcasted_iota(jnp.int32, (1, 2 * LH), 1) // H - (L - 1)

    w1_ref[...] = jnp.zeros_like(w1_ref)
    w3_ref[...] = jnp.zeros_like(w3_ref)
    klong = None
    a_chunk = []
    for dr in range(2):
        o = 3 * dr
        apr, api = powers(o)
        qr, qi = zoh(pr[o:o + 1, :], pr[o + 1:o + 2, :], pr[o + 2:o + 3, :])
        bbr_t, bbi_t = cmul(qr, qi, bt_ref[2 * dr], bt_ref[2 * dr + 1])
        lag = lag2 if dr == 0 else -lag2
        sel = sel_mat(lag)
        er, ei = _dot_sel([apr, api], sel)
        gr, gi = cmul(er, ei, lane_tile(ct_ref[2 * dr], 2 * LH), lane_tile(ct_ref[2 * dr + 1], 2 * LH))
        kd = _dot_f32(bbr_t, gr) - _dot_f32(bbi_t, gi)
        klong = kd if klong is None else klong + kd
        qrc, qic = zoh(pc[:, o:o + 1], pc[:, o + 1:o + 2], pc[:, o + 2:o + 3])
        bbr, bbi = cmul(qrc, qic, bc_ref[2 * dr], bc_ref[2 * dr + 1])
        sel = sel_mat((L - 1 - tok_lh) if dr == 0 else tok_lh)
        er, ei = _dot_sel([apr, api], sel)
        wr, wi = cmul(er, ei, lane_tile(bbr, LH), lane_tile(bbi, LH))
        w1_ref[(4 * dr) * P:(4 * dr + 1) * P, :] = wr.astype(BF16)
        w1_ref[(4 * dr + 2) * P:(4 * dr + 3) * P, :] = wi.astype(BF16)
        sel = sel_mat((tok_lh + 1) if dr == 0 else (L - tok_lh))
        er, ei = _dot_sel([apr, api], sel)
        gr, gi = cmul(er, ei, lane_tile(ct_ref[2 * dr], LH), lane_tile(ct_ref[2 * dr + 1], LH))
        w3_ref[(4 * dr) * P:(4 * dr + 1) * P, :] = gr.astype(BF16)
        w3_ref[(4 * dr + 2) * P:(4 * dr + 3) * P, :] = (-gi).astype(BF16)
        dtr = jnp.exp(pr[o + 2:o + 3, :])
        mag = jnp.exp(pr[o:o + 1, :] * dtr * L)
        ang = pr[o + 1:o + 2, :] * dtr * L
        pad = jnp.zeros((1, LANES - P), F32)
        a_chunk.append((jnp.concatenate([mag * jnp.cos(ang), pad], axis=1),
                        jnp.concatenate([mag * jnp.sin(ang), pad], axis=1)))

    for j in range(L):
        start = (L - 1 - j) * H
        t_ref[j * H:(j + 1) * H, :] = klong[:, start:start + LH].astype(BF16)

    nt = (((1,), (1,)), ((), ()))
    ul = ul_ref[...]
    uc = uc_ref[...]
    sl_ref[0:ncl, :] = lax.dot_general(ul, w1_ref[...], nt, preferred_element_type=F32)
    sl_ref[ncl:, :] = lax.dot_general(uc, w1_ref[...], nt, preferred_element_type=F32)
    sin_ref[ncl + SUBLANES:, :] = jnp.zeros((ctx_rows - SUBLANES, 4 * LANES), F32)

    (afr, afi), (abr, abi) = a_chunk
    rid = lax.broadcasted_iota(jnp.int32, (SUBLANES, LANES), 0)

    def block_scan(carry, rf0, rb0, n_rows):
        fr, fi, br, bi = carry
        lf_r, lf_i = sl_ref[pl.ds(rf0, SUBLANES), 0:LANES], sl_ref[pl.ds(rf0, SUBLANES), LANES:2 * LANES]
        lb_r = sl_ref[pl.ds(rb0, SUBLANES), 2 * LANES:3 * LANES]
        lb_i = sl_ref[pl.ds(rb0, SUBLANES), 3 * LANES:4 * LANES]
        of_r = of_i = ob_r = ob_i = jnp.zeros((SUBLANES, LANES), F32)
        for k in range(n_rows):
            r = k
            of_r = jnp.where(rid == r, fr, of_r)
            of_i = jnp.where(rid == r, fi, of_i)
            fr, fi = (afr * fr - afi * fi + lf_r[r:r + 1, :], afr * fi + afi * fr + lf_i[r:r + 1, :])
            r = n_rows - 1 - k
            ob_r = jnp.where(rid == r, br, ob_r)
            ob_i = jnp.where(rid == r, bi, ob_i)
            br, bi = (abr * br - abi * bi + lb_r[r:r + 1, :], abr * bi + abi * br + lb_i[r:r + 1, :])
        sin_ref[pl.ds(rf0, SUBLANES), 0:LANES] = of_r
        sin_ref[pl.ds(rf0, SUBLANES), LANES:2 * LANES] = of_i
        sin_ref[pl.ds(rb0, SUBLANES), 2 * LANES:3 * LANES] = ob_r
        sin_ref[pl.ds(rb0, SUBLANES), 3 * LANES:4 * LANES] = ob_i
        return fr, fi, br, bi

    z = jnp.zeros((1, LANES), F32)
    carry = block_scan((z, z, z, z), ncl, ncl, n_ctx_chunks)
    nblk = ncl // SUBLANES

    def step(b, carry):
        rf0 = pl.multiple_of(b * SUBLANES, SUBLANES)
        rb0 = pl.multiple_of((nblk - 1 - b) * SUBLANES, SUBLANES)
        return block_scan(carry, rf0, rb0, SUBLANES)

    lax.fori_loop(0, nblk, step, carry)

    yl_ref[...] = (_dot(ul, t_ref[...]) + _dot(sin_ref[0:ncl, :].astype(BF16), w3_ref[...])).astype(BF16)
    yc_ref[...] = (_dot(uc, t_ref[...]) + _dot(sin_ref[ncl:, :].astype(BF16), w3_ref[...])).astype(BF16)


def _ssm_params(a_re, a_im, log_dt, b_re, b_im, c_re, c_im):
    depth, _, G, P = a_re.shape
    ldt = jnp.broadcast_to(log_dt[:, :, :, None], (depth, 2, G, P))
    zero = jnp.zeros((depth, G, P), F32)
    rows = jnp.stack([a_re[:, 0], a_im[:, 0], ldt[:, 0], a_re[:, 1], a_im[:, 1], ldt[:, 1], zero, zero],
                     axis=2)
    b_col = jnp.stack([b_re[:, 0], b_im[:, 0], b_re[:, 1], b_im[:, 1]], axis=2)
    c_t = jnp.stack([c_re[:, 0], c_im[:, 0], c_re[:, 1], c_im[:, 1]], axis=2).transpose(0, 1, 2, 4, 3)
    rep = (1, 1, 1, 1, LANES // b_col.shape[-1])
    return dict(cols=rows.transpose(0, 1, 3, 2), rows=rows, b_col=jnp.tile(b_col, rep),
                b_t=b_col.transpose(0, 1, 2, 4, 3), c_t=jnp.tile(c_t, rep))


def _ssm(s_ctx, s_lat, sp, l):
    G, H, P, L, LH = SSM_GROUPS, SSM_GROUP, SSM_STATE, SSM_CHUNK, SSM_LH
    n_ctx, n_lat = s_ctx.shape[0], s_lat.shape[0]
    assert n_ctx % L == 0 and n_ctx // L <= SUBLANES and n_lat % (L * BF16_ROWS) == 0
    ncl, ncc = n_lat // L, n_ctx // L

    def chunked(s):
        return s.astype(BF16).reshape(-1, L, G, H).transpose(2, 0, 1, 3).reshape(G, -1, LH)

    ul = chunked(s_lat)
    uc = jnp.pad(chunked(s_ctx), ((0, 0), (0, BF16_ROWS - ncc), (0, 0)))

    def gspec(shape):
        nd = len(shape)
        return pl.BlockSpec((None,) + shape, lambda g: (g,) + (0,) * nd)

    def lspec(shape):
        nd = len(shape)
        return pl.BlockSpec((None, None) + shape, lambda g: (l, g) + (0,) * nd)

    yl, yc = pl.pallas_call(
        functools.partial(_ssm_kernel, n_ctx_chunks=ncc),
        grid=(G,),
        in_specs=[gspec((ncl, LH)), gspec((BF16_ROWS, LH)), lspec((P, 8)), lspec((8, P)), lspec((4, P, LANES)),
                  lspec((4, H, P)), lspec((4, P, LANES))],
        out_specs=[gspec((ncl, LH)), gspec((BF16_ROWS, LH))],
        out_shape=[jax.ShapeDtypeStruct((G, ncl, LH), BF16), jax.ShapeDtypeStruct((G, BF16_ROWS, LH), BF16)],
        scratch_shapes=[
            pltpu.VMEM((LH, LH), BF16),
            pltpu.VMEM((8 * P, LH), BF16),
            pltpu.VMEM((8 * P, LH), BF16),
            pltpu.VMEM((ncl + BF16_ROWS, 4 * LANES), F32),
            pltpu.VMEM((ncl + BF16_ROWS, 4 * LANES), F32),
        ],
        compiler_params=_params("arbitrary"),
        name="ssm",
    )(ul, uc, sp["cols"], sp["rows"], sp["b_col"], sp["b_t"], sp["c_t"])

    def unchunked(y):
        return y.reshape(G, -1, L, H).transpose(1, 2, 0, 3).reshape(-1, G * H)

    return unchunked(yc[:, :ncc]), unchunked(yl)


def _attn_kernel(q_ref, k_ref, v_ref, o_ref, s_ref, m_ref, l_ref, acc_ref):
    nch = k_ref.shape[0]
    m_ref[...] = jnp.full_like(m_ref, -jnp.inf)
    l_ref[...] = jnp.zeros_like(l_ref)
    acc_ref[...] = jnp.zeros_like(acc_ref)

    def scores(c, slot, h):
        s = _dot(k_ref[c], q_ref[h])
        s_ref[slot, h] = s
        return jnp.max(s, axis=0, keepdims=True)

    def consume(c, slot, h, mx):
        m_old = m_ref[h]
        m_new = jnp.maximum(m_old, mx)
        alpha = jnp.exp2(m_old - m_new)
        p = jnp.exp2(s_ref[slot, h] - m_new)
        l_ref[h] = alpha * l_ref[h] + jnp.sum(p, axis=0, keepdims=True)
        acc_ref[h] = alpha * acc_ref[h] + _dot(v_ref[c], p.astype(BF16))
        m_ref[h] = m_new

    def half(c, slot, mx):
        new = []
        for h in range(KV_REP):
            new.append(scores(c + 1, 1 - slot, h))
            consume(c, slot, h, mx[h])
        return tuple(new)

    def body(d, mx):
        return half(2 * d + 1, 1, half(2 * d, 0, mx))

    mx = tuple(scores(0, 0, h) for h in range(KV_REP))
    npair = (nch - 1) // 2
    mx = lax.fori_loop(0, npair, body, mx)
    c = 2 * npair
    if c < nch - 1:
        mx = half(c, 0, mx)
        c += 1
    for h in range(KV_REP):
        consume(c, c % 2, h, mx[h])
        o_ref[h] = (acc_ref[h] / l_ref[h]).astype(BF16)


def _key_chunk(nk):
    for tk in (1280, 1024, 512, 256):
        if nk % tk == 0:
            return tk
    raise ValueError(f"key count {nk} must be a multiple of 256")


def _attention(qp, k, vt):
    nt, _, _, tq = qp.shape
    nk = k.shape[0]
    tk = _key_chunk(nk)
    nch = nk // tk
    kr = k.reshape(nch, tk, KV_W)
    vr = vt.reshape(N_KV_HEADS, HEAD_DIM, nch, tk).transpose(0, 2, 1, 3)
    return pl.pallas_call(
        _attn_kernel,
        grid=(N_KV_HEADS, nt),
        in_specs=[
            pl.BlockSpec((None, KV_REP, KV_W, tq), lambda j, i: (i, j, 0, 0)),
            pl.BlockSpec((nch, tk, KV_W), lambda j, i: (0, 0, 0)),
            pl.BlockSpec((None, nch, HEAD_DIM, tk), lambda j, i: (j, 0, 0, 0)),
        ],
        out_specs=pl.BlockSpec((None, KV_REP, HEAD_DIM, tq), lambda j, i: (i, j, 0, 0)),
        out_shape=jax.ShapeDtypeStruct((nt, N_Q_HEADS, HEAD_DIM, tq), BF16),
        scratch_shapes=[
            pltpu.VMEM((2, KV_REP, tk, tq), F32),
            pltpu.VMEM((KV_REP, 1, tq), F32),
            pltpu.VMEM((KV_REP, 1, tq), F32),
            pltpu.VMEM((KV_REP, HEAD_DIM, tq), F32),
        ],
        compiler_params=_params("arbitrary", "arbitrary"),
        name="attention",
    )(qp, kr, vr)


def _merge_kernel(x_ref, a_ref, ap_ref, an_ref, s_ref, y_ref, o_ref, mod_ref, g_ref, wg_ref, pw_ref, ps_ref,
                  po_ref, d_ref, wglu_ref, wao_ref, wout_ref, out_ref, ext_ref, *, n_rows):
    tm = x_ref.shape[0]
    i = pl.program_id(0)
    x = x_ref[...]
    h = _modulate(x, g_ref[1:2, :], mod_ref[0:1, :], mod_ref[1:2, :]).astype(BF16)
    gates = jax.nn.sigmoid(_dot(h, wg_ref[...]))

    a = a_ref[...]
    ext_ref[0:POOL_HALO, :] = jnp.where(i > 0, ap_ref[...], 0.0)
    ext_ref[POOL_HALO:POOL_HALO + tm, :] = a
    ext_ref[POOL_HALO + tm:, :] = jnp.where(i < pl.num_programs(0) - 1, an_ref[...], 0.0)
    e = ext_ref[...]
    r = tm + 2 * POOL_HALO
    s2 = e + pltpu.roll(e, 1, 0)
    s4 = pltpu.roll(s2, 1, 0) + pltpu.roll(s2, r - 1, 0)
    s8 = pltpu.roll(s4, 2, 0) + pltpu.roll(s4, r - 2, 0)
    s16 = pltpu.roll(s8, 4, 0) + pltpu.roll(s8, r - 4, 0)
    lane_grp = lax.broadcasted_iota(jnp.int32, (1, POOL_W), 1) // POOL_GROUP
    sums = jnp.where(lane_grp == 0, s2, jnp.where(lane_grp == 1, s4, jnp.where(lane_grp == 2, s8, s16)))
    sums = sums[POOL_HALO:POOL_HALO + tm, :]
    half_w = jnp.left_shift(1, lane_grp)
    row = i * tm + lax.broadcasted_iota(jnp.int32, (tm, 1), 0)
    cnt = jnp.minimum(row + half_w, n_rows) - jnp.maximum(row - half_w, 0)
    pooled = sums / cnt.astype(F32) - a
    ya = _dot(pooled.astype(BF16), pw_ref[...]) * ps_ref[...]
    br_a = _dot(ya.astype(BF16), po_ref[...])

    y = s_ref[...] * d_ref[...] + y_ref[...].astype(F32)
    z = _dot(jax.nn.gelu(y).astype(BF16), wglu_ref[...])
    br_b = z[:, :D_MODEL] * jax.nn.sigmoid(z[:, D_MODEL:])

    o_t = o_ref[...].reshape(Q_W, tm)
    br_c = lax.dot_general(o_t, wao_ref[...], (((0,), (0,)), ((), ())), preferred_element_type=F32)

    mixed = (gates[:, :D_MODEL] * br_a + gates[:, D_MODEL:2 * D_MODEL] * br_b
             + gates[:, 2 * D_MODEL:] * br_c)
    out_ref[...] = x + mod_ref[2:3, :] * _dot(mixed.astype(BF16), wout_ref[...])


def _merge(x, a, s, y_ssm, o_attn, p, l, stream):
    n, d = x.shape
    tm = _row_tile(n)
    nb = tm // POOL_HALO
    last = n // POOL_HALO - 1
    row = lambda w: pl.BlockSpec((tm, w), lambda i: (i, 0))
    return pl.pallas_call(
        functools.partial(_merge_kernel, n_rows=n),
        grid=(n // tm,),
        in_specs=[
            row(d), row(POOL_W),
            pl.BlockSpec((POOL_HALO, POOL_W), lambda i: (jnp.maximum(i * nb - 1, 0), 0)),
            pl.BlockSpec((POOL_HALO, POOL_W), lambda i: (jnp.minimum((i + 1) * nb, last), 0)),
            row(SSM_W), row(SSM_W),
            pl.BlockSpec((None, N_Q_HEADS, HEAD_DIM, tm), lambda i: (i, 0, 0, 0)),
            _pick((None, None, None, 3, d), l, stream, 1, 0, 0),
            _pick((None, 3, d), l, 0, 0),
            _pick((None, d, 3 * d), l, 0, 0),
            _pick((None, POOL_W, POOL_W), l, 0, 0),
            _pick((None, 1, POOL_W), l, 0, 0),
            _pick((None, POOL_W, d), l, 0, 0),
            _pick((None, 1, SSM_W), l, 0, 0),
            _pick((None, SSM_W, 2 * d), l, 0, 0),
            _pick((None, Q_W, d), l, 0, 0),
            _pick((None, d, d), l, 0, 0),
        ],
        out_specs=row(d),
        out_shape=jax.ShapeDtypeStruct((n, d), F32),
        scratch_shapes=[pltpu.VMEM((tm + 2 * POOL_HALO, POOL_W), F32)],
        compiler_params=_params("arbitrary"),
        name="merge",
    )(x, a, a, a, s, y_ssm, o_attn, p["mods"], p["norm_g"], p["w_g"], p["pool_bd"], p["pool_scale"],
      p["pool_out"], p["ssm_d"], p["w_glu"], p["w_ao"], p["w_out"])


def kernel(x, c, ctx, c_ctx, mod_w, mod_b, norm_g, ffn_w13, ffn_w2, w_in, pool_w, pool_scale, pool_out, ssm_a_re, ssm_a_im, ssm_log_dt, ssm_b_re, ssm_b_im, ssm_c_re, ssm_c_im, ssm_d, ssm_glu_w, q_norm_g, k_norm_g, attn_out, w_out, final_norm_g):
    batch, n, d = x.shape
    depth = mod_w.shape[0]
    assert batch == 1 and d == D_MODEL and n % GRID_W == 0
    xl = x[0]
    xc = ctx[0]

    cond8 = jnp.concatenate([c, c_ctx[None], jnp.zeros((SUBLANES - 2, d), F32)], axis=0)
    n_groups = POOL_W // POOL_GROUP
    pool_bd = (jnp.eye(n_groups, dtype=F32)[None, :, None, :, None] * pool_w[:, :, :, None, :]
               ).reshape(depth, POOL_W, POOL_W)
    gn = jnp.stack([q_norm_g * Q_SCALE, k_norm_g], axis=1)
    p = dict(
        mods=_adaln(cond8, mod_w, mod_b).reshape(depth, SUBLANES, 3, 3, d),
        norm_g=norm_g,
        w13=ffn_w13.astype(BF16),
        w2=ffn_w2.astype(BF16),
        w_s=w_in[:, :, :POOL_W + SSM_W].astype(BF16),
        w_t=w_in[:, :, POOL_W + SSM_W:N_PROJ].transpose(0, 2, 1).astype(BF16),
        w_g=w_in[:, :, N_PROJ:].astype(BF16),
        gn=jnp.broadcast_to(gn[:, :, :, None], (depth, 2, HEAD_DIM, ROW_TILE)),
        pool_bd=pool_bd.astype(BF16),
        pool_scale=pool_scale.reshape(depth, 1, POOL_W),
        pool_out=pool_out.astype(BF16),
        ssm_d=ssm_d.reshape(depth, 1, SSM_W),
        w_glu=ssm_glu_w.astype(BF16),
        w_ao=attn_out.astype(BF16),
        w_out=w_out.astype(BF16),
    )
    sp = _ssm_params(ssm_a_re, ssm_a_im, ssm_log_dt, ssm_b_re, ssm_b_im, ssm_c_re, ssm_c_im)
    rope_tab = _rope_table(n)
    LAT, CTX = 0, 1

    for l in range(depth):
        with_ctx_out = l < depth - 1
        xl = _ffn(xl, p, l, LAT, 0)
        xc = _ffn(xc, p, l, CTX, 0)
        a_l, s_l, q_l, k_l, v_l = _inproj(xl, p, l, LAT, rope_tab)
        a_c, s_c, q_c, k_c, v_c = _inproj(xc, p, l, CTX)
        y_c, y_l = _ssm(s_c, s_l, sp, l)
        o_l = _attention(q_l, jnp.concatenate([k_l, k_c], axis=0), jnp.concatenate([v_l, v_c], axis=2))
        xl = _merge(xl, a_l, s_l, y_l, o_l, p, l, LAT)
        if with_ctx_out:
            o_c = _attention(q_c, k_c, v_c)
            xc = _merge(xc, a_c, s_c, y_c, o_c, p, l, CTX)
        last = l == depth - 1
        xl = _ffn(xl, p, l, LAT, 1, final_norm_g if last else None)
        if with_ctx_out:
            xc = _ffn(xc, p, l, CTX, 1)
    return xl[None]
```

```python
import functools
import math

import jax
import jax.numpy as jnp
from jax import lax
from jax.experimental import pallas as pl
from jax.experimental.pallas import tpu as pltpu

F32 = jnp.float32
BF16 = jnp.bfloat16

D_MODEL = 1024
N_MOD = 9
D_FF = 2816
GRID_W = 64
POOL_GROUP = 64
POOL_W = 256
SSM_W = 256
SSM_GROUP = 16
SSM_GROUPS = 16
SSM_STATE = 64
HEAD_DIM = 64
N_Q_HEADS = 8
N_KV_HEADS = 2
KV_REP = N_Q_HEADS // N_KV_HEADS
Q_W = N_Q_HEADS * HEAD_DIM
KV_W = N_KV_HEADS * HEAD_DIM
ROPE_THETA = 10000.0
EPS = 1e-6
N_PROJ = POOL_W + SSM_W + Q_W + 2 * KV_W
QK_W = Q_W + KV_W
ROT = HEAD_DIM // 4

LANES = 128
SUBLANES = 8
BF16_ROWS = 16
VMEM_LIMIT_BYTES = 56 * 1024 * 1024

ROW_TILE = 512
SSM_CHUNK = 64
SSM_LH = SSM_CHUNK * SSM_GROUP
POOL_HALO = 8
ATTN_UNROLL = 4
Q_SCALE = HEAD_DIM ** -0.5 * math.log2(math.e)


def _row_tile(n):
    for t in (ROW_TILE, ROW_TILE // 2):
        if n % t == 0:
            return t
    raise ValueError(f"row count {n} must be a multiple of {ROW_TILE // 2}")


def _pick(block, *index):
    return pl.BlockSpec(block, lambda *_: index, pipeline_mode=pl.Buffered(1))


def _params(*sem):
    return pltpu.CompilerParams(dimension_semantics=sem, vmem_limit_bytes=VMEM_LIMIT_BYTES)


def _modulate(x, g, shift, scale):
    ms = jnp.mean(x * x, axis=-1, keepdims=True)
    xn = x * lax.rsqrt(ms + EPS) * g
    return xn * (1.0 + scale) + shift


def _split2(x):
    hi = x.astype(BF16)
    lo = (x - hi.astype(F32)).astype(BF16)
    return hi, lo


def _dot(a, b):
    return jnp.dot(a, b, preferred_element_type=F32)


def _dot_sel(xs, sel):
    rows = xs[0].shape[0]
    parts = [part for x in xs for part in _split2(x)]
    out = _dot(jnp.concatenate(parts, axis=0), sel)
    return [out[2 * i * rows:(2 * i + 1) * rows] + out[(2 * i + 1) * rows:(2 * i + 2) * rows]
            for i in range(len(xs))]


def _dot_f32(a, b):
    ah, al = _split2(a)
    bh, bl = _split2(b)
    return _dot(ah, bh) + _dot(ah, bl) + _dot(al, bh)


def _adaln_kernel(c_ref, w_ref, b_ref, o_ref):
    c = c_ref[...]
    cs = c * jax.nn.sigmoid(c)
    o_ref[...] = _dot_f32(cs, w_ref[...]) + b_ref[...]


def _adaln(cond8, mod_w, mod_b):
    depth, d, nm = mod_w.shape
    tn = 1024
    return pl.pallas_call(
        _adaln_kernel,
        grid=(depth, nm // tn),
        in_specs=[
            pl.BlockSpec((SUBLANES, d), lambda l, j: (0, 0)),
            pl.BlockSpec((None, d, tn), lambda l, j: (l, 0, j)),
            pl.BlockSpec((None, 1, tn), lambda l, j: (l, 0, j)),
        ],
        out_specs=pl.BlockSpec((None, SUBLANES, tn), lambda l, j: (l, 0, j)),
        out_shape=jax.ShapeDtypeStruct((depth, SUBLANES, nm), F32),
        compiler_params=_params("arbitrary", "arbitrary"),
        name="adaln",
    )(cond8, mod_w, mod_b.reshape(depth, 1, nm))


def _ffn_kernel(x_ref, mod_ref, g_ref, w13_ref, w2_ref, *rest, norm_row, final):
    o_ref = rest[-1]
    x = x_ref[...]
    h = _modulate(x, g_ref[norm_row:norm_row + 1, :], mod_ref[0:1, :], mod_ref[1:2, :]).astype(BF16)
    g = _dot(h, w13_ref[:, :D_FF])
    u = _dot(h, w13_ref[:, D_FF:])
    acc = _dot((g * jax.nn.sigmoid(g) * u).astype(BF16), w2_ref[...])
    y = x + (0.5 * mod_ref[2:3, :]) * acc
    if final:
        fg_ref = rest[0]
        y = y * lax.rsqrt(jnp.mean(y * y, axis=-1, keepdims=True) + EPS) * fg_ref[...]
    o_ref[...] = y


def _ffn(x, p, l, stream, which, final_g=None):
    n, d = x.shape
    tm = _row_tile(n)
    final = final_g is not None
    in_specs = [
        pl.BlockSpec((tm, d), lambda i: (i, 0)),
        _pick((None, None, None, 3, d), l, stream, 2 * which, 0, 0),
        _pick((None, 3, d), l, 0, 0),
        _pick((None, None, d, 2 * D_FF), l, which, 0, 0),
        _pick((None, None, D_FF, d), l, which, 0, 0),
    ]
    args = [x, p["mods"], p["norm_g"], p["w13"], p["w2"]]
    if final:
        in_specs.append(_pick((1, d), 0, 0))
        args.append(final_g.reshape(1, d))
    return pl.pallas_call(
        functools.partial(_ffn_kernel, norm_row=2 * which, final=final),
        grid=(n // tm,),
        in_specs=in_specs,
        out_specs=pl.BlockSpec((tm, d), lambda i: (i, 0)),
        out_shape=jax.ShapeDtypeStruct((n, d), F32),
        compiler_params=_params("arbitrary"),
        name="ffn_final" if final else "ffn",
    )(*args)


def _inproj_kernel(x_ref, mod_ref, g_ref, ws_ref, wt_ref, gn_ref, *rest, rope):
    if rope:
        rope_ref = rest[0]
        rest = rest[1:]
    a_ref, s_ref, q_ref, k_ref, v_ref = rest
    tm = x_ref.shape[0]
    x = x_ref[...]
    h = _modulate(x, g_ref[1:2, :], mod_ref[0:1, :], mod_ref[1:2, :]).astype(BF16)
    ps = _dot(h, ws_ref[...])
    a_ref[...] = ps[:, 0:POOL_W]
    s_ref[...] = ps[:, POOL_W:]
    pt = lax.dot_general(wt_ref[...], h, (((1,), (1,)), ((), ())), preferred_element_type=F32)
    zeros = jnp.zeros((HEAD_DIM, tm), BF16)
    k_heads = []
    for hd in range(N_Q_HEADS + N_KV_HEADS):
        blk = pt[hd * HEAD_DIM:(hd + 1) * HEAD_DIM, :]
        ss = jnp.sum(blk * blk, axis=0, keepdims=True)
        xn = blk * lax.rsqrt(ss * (1.0 / HEAD_DIM) + EPS) * gn_ref[0 if hd < N_Q_HEADS else 1]
        if rope:
            x0, x1, x2, x3 = (xn[i * ROT:(i + 1) * ROT, :] for i in range(4))
            ca, sa, cb, sb = (rope_ref[i * ROT:(i + 1) * ROT, :] for i in range(4))
            xn = jnp.concatenate([x0 * ca - x1 * sa, x1 * ca + x0 * sa, x2 * cb - x3 * sb, x3 * cb + x2 * sb], axis=0)
        if hd < N_Q_HEADS:
            qb = xn.astype(BF16)
            q_ref[hd] = jnp.concatenate([qb, zeros] if hd < KV_REP else [zeros, qb], axis=0)
        else:
            k_heads.append(xn)
    k_ref[...] = jnp.concatenate(k_heads, axis=0).T.astype(BF16)
    v_ref[...] = pt[QK_W:, :].reshape(N_KV_HEADS, HEAD_DIM, tm).astype(BF16)


def _inproj(x, p, l, stream, rope_tab=None):
    n, d = x.shape
    tm = _row_tile(n)
    rope = rope_tab is not None
    in_specs = [
        pl.BlockSpec((tm, d), lambda i: (i, 0)),
        _pick((None, None, None, 3, d), l, stream, 1, 0, 0),
        _pick((None, 3, d), l, 0, 0),
        _pick((None, d, POOL_W + SSM_W), l, 0, 0),
        _pick((None, QK_W + KV_W, d), l, 0, 0),
        _pick((None, 2, HEAD_DIM, tm), l, 0, 0, 0),
    ]
    args = [x, p["mods"], p["norm_g"], p["w_s"], p["w_t"], p["gn"]]
    if rope:
        in_specs.append(pl.BlockSpec((HEAD_DIM, tm), lambda i: (0, i)))
        args.append(rope_tab)
    return pl.pallas_call(
        functools.partial(_inproj_kernel, rope=rope),
        grid=(n // tm,),
        in_specs=in_specs,
        out_specs=[
            pl.BlockSpec((tm, POOL_W), lambda i: (i, 0)),
            pl.BlockSpec((tm, SSM_W), lambda i: (i, 0)),
            pl.BlockSpec((None, N_Q_HEADS, KV_W, tm), lambda i: (i, 0, 0, 0)),
            pl.BlockSpec((tm, KV_W), lambda i: (i, 0)),
            pl.BlockSpec((N_KV_HEADS, HEAD_DIM, tm), lambda i: (0, 0, i)),
        ],
        out_shape=[
            jax.ShapeDtypeStruct((n, POOL_W), F32),
            jax.ShapeDtypeStruct((n, SSM_W), F32),
            jax.ShapeDtypeStruct((n // tm, N_Q_HEADS, KV_W, tm), BF16),
            jax.ShapeDtypeStruct((n, KV_W), BF16),
            jax.ShapeDtypeStruct((N_KV_HEADS, HEAD_DIM, n), BF16),
        ],
        compiler_params=_params("arbitrary"),
        name="inproj_rope" if rope else "inproj",
    )(*args)


def _rope_table(n):
    t = jnp.arange(n)
    rows = (t // GRID_W).astype(F32)
    cols = (t % GRID_W).astype(F32)
    half = HEAD_DIM // 2
    inv = ROPE_THETA ** (-jnp.arange(0, half, 2, dtype=F32) / half)
    ang_r = inv[:, None] * rows[None, :]
    ang_c = inv[:, None] * cols[None, :]
    return jnp.concatenate([jnp.cos(ang_r), jnp.sin(ang_r), jnp.cos(ang_c), jnp.sin(ang_c)], axis=0)


def _ssm_kernel(ul_ref, uc_ref, pc_ref, pr_ref, bc_ref, bt_ref, ct_ref, yl_ref, yc_ref,
                t_ref, w1_ref, w3_ref, sl_ref, sin_ref, *, n_ctx_chunks):
    L, H, P, LH = SSM_CHUNK, SSM_GROUP, SSM_STATE, SSM_LH
    ncl = ul_ref.shape[0]
    ctx_rows = uc_ref.shape[0]
    pc = pc_ref[...]
    pr = pr_ref[...]

    def powers(o):
        dt = jnp.exp(pc[:, o + 2:o + 3])
        zr = pc[:, o:o + 1] * dt
        zi = pc[:, o + 1:o + 2] * dt
        m = lax.broadcasted_iota(jnp.int32, (1, LANES), 1).astype(F32)
        mag = jnp.exp(zr * m)
        return mag * jnp.cos(zi * m), mag * jnp.sin(zi * m)

    def zoh(lr, li, ldt):
        dt = jnp.exp(ldt)
        mag = jnp.exp(lr * dt)
        nr = mag * jnp.cos(li * dt) - 1.0
        ni = mag * jnp.sin(li * dt)
        den = lr * lr + li * li
        return (nr * lr + ni * li) / den, (ni * lr - nr * li) / den

    def sel_mat(expo):
        w = expo.shape[1]
        return (lax.broadcasted_iota(jnp.int32, (LANES, w), 0) == expo).astype(BF16)

    def lane_tile(x, w):
        return jnp.concatenate([x] * (w // LANES), axis=1)

    def cmul(ar, ai, br, bi):
        return ar * br - ai * bi, ar * bi + ai * br

    tok_lh = lax.broadcasted_iota(jnp.int32, (1, LH), 1) // H
    lag2 = lax.broadcasted_iota(jnp.int32, (1, 2 * LH), 1) // H - (L - 1)

    w1_ref[...] = jnp.zeros_like(w1_ref)
    w3_ref[...] = jnp.zeros_like(w3_ref)
    klong = None
    a_chunk = []
    for dr in range(2):
        o = 3 * dr
        apr, api = powers(o)
        qr, qi = zoh(pr[o:o + 1, :], pr[o + 1:o + 2, :], pr[o + 2:o + 3, :])
        bbr_t, bbi_t = cmul(qr, qi, bt_ref[2 * dr], bt_ref[2 * dr + 1])
        lag = lag2 if dr == 0 else -lag2
        sel = sel_mat(lag)
        er, ei = _dot_sel([apr, api], sel)
        gr, gi = cmul(er, ei, lane_tile(ct_ref[2 * dr], 2 * LH), lane_tile(ct_ref[2 * dr + 1], 2 * LH))
        kd = _dot_f32(bbr_t, gr) - _dot_f32(bbi_t, gi)
        klong = kd if klong is None else klong + kd
        qrc, qic = zoh(pc[:, o:o + 1], pc[:, o + 1:o + 2], pc[:, o + 2:o + 3])
        bbr, bbi = cmul(qrc, qic, bc_ref[2 * dr], bc_ref[2 * dr + 1])
        sel = sel_mat((L - 1 - tok_lh) if dr == 0 else tok_lh)
        er, ei = _dot_sel([apr, api], sel)
        wr, wi = cmul(er, ei, lane_tile(bbr, LH), lane_tile(bbi, LH))
        w1_ref[(4 * dr) * P:(4 * dr + 1) * P, :] = wr.astype(BF16)
        w1_ref[(4 * dr + 2) * P:(4 * dr + 3) * P, :] = wi.astype(BF16)
        sel = sel_mat((tok_lh + 1) if dr == 0 else (L - tok_lh))
        er, ei = _dot_sel([apr, api], sel)
        gr, gi = cmul(er, ei, lane_tile(ct_ref[2 * dr], LH), lane_tile(ct_ref[2 * dr + 1], LH))
        w3_ref[(4 * dr) * P:(4 * dr + 1) * P, :] = gr.astype(BF16)
        w3_ref[(4 * dr + 2) * P:(4 * dr + 3) * P, :] = (-gi).astype(BF16)
        dtr = jnp.exp(pr[o + 2:o + 3, :])
        mag = jnp.exp(pr[o:o + 1, :] * dtr * L)
        ang = pr[o + 1:o + 2, :] * dtr * L
        pad = jnp.zeros((1, LANES - P), F32)
        a_chunk.append((jnp.concatenate([mag * jnp.cos(ang), pad], axis=1),
                        jnp.concatenate([mag * jnp.sin(ang), pad], axis=1)))

    for j in range(L):
        start = (L - 1 - j) * H
        t_ref[j * H:(j + 1) * H, :] = klong[:, start:start + LH].astype(BF16)

    nt = (((1,), (1,)), ((), ()))
    ul = ul_ref[...]
    uc = uc_ref[...]
    sl_ref[0:ncl, :] = lax.dot_general(ul, w1_ref[...], nt, preferred_element_type=F32)
    sl_ref[ncl:, :] = lax.dot_general(uc, w1_ref[...], nt, preferred_element_type=F32)
    sin_ref[ncl + SUBLANES:, :] = jnp.zeros((ctx_rows - SUBLANES, 4 * LANES), F32)

    (afr, afi), (abr, abi) = a_chunk
    rid = lax.broadcasted_iota(jnp.int32, (SUBLANES, LANES), 0)

    def block_scan(carry, rf0, rb0, n_rows):
        fr, fi, br, bi = carry
        lf_r, lf_i = sl_ref[pl.ds(rf0, SUBLANES), 0:LANES], sl_ref[pl.ds(rf0, SUBLANES), LANES:2 * LANES]
        lb_r = sl_ref[pl.ds(rb0, SUBLANES), 2 * LANES:3 * LANES]
        lb_i = sl_ref[pl.ds(rb0, SUBLANES), 3 * LANES:4 * LANES]
        of_r = of_i = ob_r = ob_i = jnp.zeros((SUBLANES, LANES), F32)
        for k in range(n_rows):
            r = k
            of_r = jnp.where(rid == r, fr, of_r)
            of_i = jnp.where(rid == r, fi, of_i)
            fr, fi = (afr * fr - afi * fi + lf_r[r:r + 1, :], afr * fi + afi * fr + lf_i[r:r + 1, :])
            r = n_rows - 1 - k
            ob_r = jnp.where(rid == r, br, ob_r)
            ob_i = jnp.where(rid == r, bi, ob_i)
            br, bi = (abr * br - abi * bi + lb_r[r:r + 1, :], abr * bi + abi * br + lb_i[r:r + 1, :])
        sin_ref[pl.ds(rf0, SUBLANES), 0:LANES] = of_r
        sin_ref[pl.ds(rf0, SUBLANES), LANES:2 * LANES] = of_i
        sin_ref[pl.ds(rb0, SUBLANES), 2 * LANES:3 * LANES] = ob_r
        sin_ref[pl.ds(rb0, SUBLANES), 3 * LANES:4 * LANES] = ob_i
        return fr, fi, br, bi

    z = jnp.zeros((1, LANES), F32)
    carry = block_scan((z, z, z, z), ncl, ncl, n_ctx_chunks)
    nblk = ncl // SUBLANES

    def step(b, carry):
        rf0 = pl.multiple_of(b * SUBLANES, SUBLANES)
        rb0 = pl.multiple_of((nblk - 1 - b) * SUBLANES, SUBLANES)
        return block_scan(carry, rf0, rb0, SUBLANES)

    lax.fori_loop(0, nblk, step, carry)

    yl_ref[...] = (_dot(ul, t_ref[...]) + _dot(sin_ref[0:ncl, :].astype(BF16), w3_ref[...])).astype(BF16)
    yc_ref[...] = (_dot(uc, t_ref[...]) + _dot(sin_ref[ncl:, :].astype(BF16), w3_ref[...])).astype(BF16)


def _ssm_params(a_re, a_im, log_dt, b_re, b_im, c_re, c_im):
    depth, _, G, P = a_re.shape
    ldt = jnp.broadcast_to(log_dt[:, :, :, None], (depth, 2, G, P))
    zero = jnp.zeros((depth, G, P), F32)
    rows = jnp.stack([a_re[:, 0], a_im[:, 0], ldt[:, 0], a_re[:, 1], a_im[:, 1], ldt[:, 1], zero, zero],
                     axis=2)
    b_col = jnp.stack([b_re[:, 0], b_im[:, 0], b_re[:, 1], b_im[:, 1]], axis=2)
    c_t = jnp.stack([c_re[:, 0], c_im[:, 0], c_re[:, 1], c_im[:, 1]], axis=2).transpose(0, 1, 2, 4, 3)
    rep = (1, 1, 1, 1, LANES // b_col.shape[-1])
    return dict(cols=rows.transpose(0, 1, 3, 2), rows=rows, b_col=jnp.tile(b_col, rep),
                b_t=b_col.transpose(0, 1, 2, 4, 3), c_t=jnp.tile(c_t, rep))


def _ssm(s_ctx, s_lat, sp, l):
    G, H, P, L, LH = SSM_GROUPS, SSM_GROUP, SSM_STATE, SSM_CHUNK, SSM_LH
    n_ctx, n_lat = s_ctx.shape[0], s_lat.shape[0]
    assert n_ctx % L == 0 and n_ctx // L <= SUBLANES and n_lat % (L * BF16_ROWS) == 0
    ncl, ncc = n_lat // L, n_ctx // L

    def chunked(s):
        return s.astype(BF16).reshape(-1, L, G, H).transpose(2, 0, 1, 3).reshape(G, -1, LH)

    ul = chunked(s_lat)
    uc = jnp.pad(chunked(s_ctx), ((0, 0), (0, BF16_ROWS - ncc), (0, 0)))

    def gspec(shape):
        nd = len(shape)
        return pl.BlockSpec((None,) + shape, lambda g: (g,) + (0,) * nd)

    def lspec(shape):
        nd = len(shape)
        return pl.BlockSpec((None, None) + shape, lambda g: (l, g) + (0,) * nd)

    yl, yc = pl.pallas_call(
        functools.partial(_ssm_kernel, n_ctx_chunks=ncc),
        grid=(G,),
        in_specs=[gspec((ncl, LH)), gspec((BF16_ROWS, LH)), lspec((P, 8)), lspec((8, P)), lspec((4, P, LANES)),
                  lspec((4, H, P)), lspec((4, P, LANES))],
        out_specs=[gspec((ncl, LH)), gspec((BF16_ROWS, LH))],
        out_shape=[jax.ShapeDtypeStruct((G, ncl, LH), BF16), jax.ShapeDtypeStruct((G, BF16_ROWS, LH), BF16)],
        scratch_shapes=[
            pltpu.VMEM((LH, LH), BF16),
            pltpu.VMEM((8 * P, LH), BF16),
            pltpu.VMEM((8 * P, LH), BF16),
            pltpu.VMEM((ncl + BF16_ROWS, 4 * LANES), F32),
            pltpu.VMEM((ncl + BF16_ROWS, 4 * LANES), F32),
        ],
        compiler_params=_params("arbitrary"),
        name="ssm",
    )(ul, uc, sp["cols"], sp["rows"], sp["b_col"], sp["b_t"], sp["c_t"])

    def unchunked(y):
        return y.reshape(G, -1, L, H).transpose(1, 2, 0, 3).reshape(-1, G * H)

    return unchunked(yc[:, :ncc]), unchunked(yl)


def _attn_kernel(q_ref, k_ref, v_ref, o_ref, s_ref, m_ref, l_ref, acc_ref):
    nch = k_ref.shape[0]
    m_ref[...] = jnp.full_like(m_ref, -jnp.inf)
    l_ref[...] = jnp.zeros_like(l_ref)
    acc_ref[...] = jnp.zeros_like(acc_ref)

    def scores(c, slot, h):
        s = _dot(k_ref[c], q_ref[h])
        s_ref[slot, h] = s
        return jnp.max(s, axis=0, keepdims=True)

    def consume(c, slot, h, mx):
        m_old = m_ref[h]
        m_new = jnp.maximum(m_old, mx)
        alpha = jnp.exp2(m_old - m_new)
        p = jnp.exp2(s_ref[slot, h] - m_new)
        l_ref[h] = alpha * l_ref[h] + jnp.sum(p, axis=0, keepdims=True)
        acc_ref[h] = alpha * acc_ref[h] + _dot(v_ref[c], p.astype(BF16))
        m_ref[h] = m_new

    def half(c, slot, mx):
        new = []
        for h in range(KV_REP):
            new.append(scores(c + 1, 1 - slot, h))
            consume(c, slot, h, mx[h])
        return tuple(new)

    def body(d, mx):
        for u in range(ATTN_UNROLL):
            mx = half(ATTN_UNROLL * d + u, u % 2, mx)
        return mx

    mx = tuple(scores(0, 0, h) for h in range(KV_REP))
    nloop = (nch - 1) // ATTN_UNROLL
    mx = lax.fori_loop(0, nloop, body, mx)
    c = ATTN_UNROLL * nloop
    while c < nch - 1:
        mx = half(c, c % 2, mx)
        c += 1
    for h in range(KV_REP):
        consume(c, c % 2, h, mx[h])
        o_ref[h] = (acc_ref[h] / l_ref[h]).astype(BF16)


def _key_chunk(nk):
    for tk in (640, 512, 256):
        if nk % tk == 0:
            return tk
    raise ValueError(f"key count {nk} must be a multiple of 256")


def _attention(qp, k, vt):
    nt, _, _, tq = qp.shape
    nk = k.shape[0]
    tk = _key_chunk(nk)
    nch = nk // tk
    kr = k.reshape(nch, tk, KV_W)
    vr = vt.reshape(N_KV_HEADS, HEAD_DIM, nch, tk).transpose(0, 2, 1, 3)
    return pl.pallas_call(
        _attn_kernel,
        grid=(N_KV_HEADS, nt),
        in_specs=[
            pl.BlockSpec((None, KV_REP, KV_W, tq), lambda j, i: (i, j, 0, 0)),
            pl.BlockSpec((nch, tk, KV_W), lambda j, i: (0, 0, 0)),
            pl.BlockSpec((None, nch, HEAD_DIM, tk), lambda j, i: (j, 0, 0, 0)),
        ],
        out_specs=pl.BlockSpec((None, KV_REP, HEAD_DIM, tq), lambda j, i: (i, j, 0, 0)),
        out_shape=jax.ShapeDtypeStruct((nt, N_Q_HEADS, HEAD_DIM, tq), BF16),
        scratch_shapes=[
            pltpu.VMEM((2, KV_REP, tk, tq), F32),
            pltpu.VMEM((KV_REP, 1, tq), F32),
            pltpu.VMEM((KV_REP, 1, tq), F32),
            pltpu.VMEM((KV_REP, HEAD_DIM, tq), F32),
        ],
        compiler_params=_params("arbitrary", "arbitrary"),
        name="attention",
    )(qp, kr, vr)


def _merge_kernel(x_ref, a_ref, ap_ref, an_ref, s_ref, y_ref, o_ref, mod_ref, g_ref, wg_ref, pw_ref, ps_ref,
                  po_ref, d_ref, wglu_ref, wao_ref, wout_ref, out_ref, ext_ref, *, n_rows):
    tm = x_ref.shape[0]
    i = pl.program_id(0)
    x = x_ref[...]
    h = _modulate(x, g_ref[1:2, :], mod_ref[0:1, :], mod_ref[1:2, :]).astype(BF16)
    gates = jax.nn.sigmoid(_dot(h, wg_ref[...]))

    a = a_ref[...]
    ext_ref[0:POOL_HALO, :] = jnp.where(i > 0, ap_ref[...], 0.0)
    ext_ref[POOL_HALO:POOL_HALO + tm, :] = a
    ext_ref[POOL_HALO + tm:, :] = jnp.where(i < pl.num_programs(0) - 1, an_ref[...], 0.0)
    e = ext_ref[...]
    r = tm + 2 * POOL_HALO
    s2 = e + pltpu.roll(e, 1, 0)
    s4 = pltpu.roll(s2, 1, 0) + pltpu.roll(s2, r - 1, 0)
    s8 = pltpu.roll(s4, 2, 0) + pltpu.roll(s4, r - 2, 0)
    s16 = pltpu.roll(s8, 4, 0) + pltpu.roll(s8, r - 4, 0)
    lane_grp = lax.broadcasted_iota(jnp.int32, (1, POOL_W), 1) // POOL_GROUP
    sums = jnp.where(lane_grp == 0, s2, jnp.where(lane_grp == 1, s4, jnp.where(lane_grp == 2, s8, s16)))
    sums = sums[POOL_HALO:POOL_HALO + tm, :]
    half_w = jnp.left_shift(1, lane_grp)
    row = i * tm + lax.broadcasted_iota(jnp.int32, (tm, 1), 0)
    cnt = jnp.minimum(row + half_w, n_rows) - jnp.maximum(row - half_w, 0)
    pooled = sums / cnt.astype(F32) - a
    ya = _dot(pooled.astype(BF16), pw_ref[...]) * ps_ref[...]
    br_a = _dot(ya.astype(BF16), po_ref[...])

    y = s_ref[...] * d_ref[...] + y_ref[...].astype(F32)
    z = _dot(jax.nn.gelu(y).astype(BF16), wglu_ref[...])
    br_b = z[:, :D_MODEL] * jax.nn.sigmoid(z[:, D_MODEL:])

    o_t = o_ref[...].reshape(Q_W, tm)
    br_c = lax.dot_general(o_t, wao_ref[...], (((0,), (0,)), ((), ())), preferred_element_type=F32)

    mixed = (gates[:, :D_MODEL] * br_a + gates[:, D_MODEL:2 * D_MODEL] * br_b
             + gates[:, 2 * D_MODEL:] * br_c)
    out_ref[...] = x + mod_ref[2:3, :] * _dot(mixed.astype(BF16), wout_ref[...])


def _merge(x, a, s, y_ssm, o_attn, p, l, stream):
    n, d = x.shape
    tm = _row_tile(n)
    nb = tm // POOL_HALO
    last = n // POOL_HALO - 1
    row = lambda w: pl.BlockSpec((tm, w), lambda i: (i, 0))
    return pl.pallas_call(
        functools.partial(_merge_kernel, n_rows=n),
        grid=(n // tm,),
        in_specs=[
            row(d), row(POOL_W),
            pl.BlockSpec((POOL_HALO, POOL_W), lambda i: (jnp.maximum(i * nb - 1, 0), 0)),
            pl.BlockSpec((POOL_HALO, POOL_W), lambda i: (jnp.minimum((i + 1) * nb, last), 0)),
            row(SSM_W), row(SSM_W),
            pl.BlockSpec((None, N_Q_HEADS, HEAD_DIM, tm), lambda i: (i, 0, 0, 0)),
            _pick((None, None, None, 3, d), l, stream, 1, 0, 0),
            _pick((None, 3, d), l, 0, 0),
            _pick((None, d, 3 * d), l, 0, 0),
            _pick((None, POOL_W, POOL_W), l, 0, 0),
            _pick((None, 1, POOL_W), l, 0, 0),
            _pick((None, POOL_W, d), l, 0, 0),
            _pick((None, 1, SSM_W), l, 0, 0),
            _pick((None, SSM_W, 2 * d), l, 0, 0),
            _pick((None, Q_W, d), l, 0, 0),
            _pick((None, d, d), l, 0, 0),
        ],
        out_specs=row(d),
        out_shape=jax.ShapeDtypeStruct((n, d), F32),
        scratch_shapes=[pltpu.VMEM((tm + 2 * POOL_HALO, POOL_W), F32)],
        compiler_params=_params("arbitrary"),
        name="merge",
    )(x, a, a, a, s, y_ssm, o_attn, p["mods"], p["norm_g"], p["w_g"], p["pool_bd"], p["pool_scale"],
      p["pool_out"], p["ssm_d"], p["w_glu"], p["w_ao"], p["w_out"])


def kernel(x, c, ctx, c_ctx, mod_w, mod_b, norm_g, ffn_w13, ffn_w2, w_in, pool_w, pool_scale, pool_out, ssm_a_re, ssm_a_im, ssm_log_dt, ssm_b_re, ssm_b_im, ssm_c_re, ssm_c_im, ssm_d, ssm_glu_w, q_norm_g, k_norm_g, attn_out, w_out, final_norm_g):
    batch, n, d = x.shape
    depth = mod_w.shape[0]
    assert batch == 1 and d == D_MODEL and n % GRID_W == 0
    xl = x[0]
    xc = ctx[0]

    cond8 = jnp.concatenate([c, c_ctx[None], jnp.zeros((SUBLANES - 2, d), F32)], axis=0)
    n_groups = POOL_W // POOL_GROUP
    pool_bd = (jnp.eye(n_groups, dtype=F32)[None, :, None, :, None] * pool_w[:, :, :, None, :]
               ).reshape(depth, POOL_W, POOL_W)
    gn = jnp.stack([q_norm_g * Q_SCALE, k_norm_g], axis=1)
    p = dict(
        mods=_adaln(cond8, mod_w, mod_b).reshape(depth, SUBLANES, 3, 3, d),
        norm_g=norm_g,
        w13=ffn_w13.astype(BF16),
        w2=ffn_w2.astype(BF16),
        w_s=w_in[:, :, :POOL_W + SSM_W].astype(BF16),
        w_t=w_in[:, :, POOL_W + SSM_W:N_PROJ].transpose(0, 2, 1).astype(BF16),
        w_g=w_in[:, :, N_PROJ:].astype(BF16),
        gn=jnp.broadcast_to(gn[:, :, :, None], (depth, 2, HEAD_DIM, ROW_TILE)),
        pool_bd=pool_bd.astype(BF16),
        pool_scale=pool_scale.reshape(depth, 1, POOL_W),
        pool_out=pool_out.astype(BF16),
        ssm_d=ssm_d.reshape(depth, 1, SSM_W),
        w_glu=ssm_glu_w.astype(BF16),
        w_ao=attn_out.astype(BF16),
        w_out=w_out.astype(BF16),
    )
    sp = _ssm_params(ssm_a_re, ssm_a_im, ssm_log_dt, ssm_b_re, ssm_b_im, ssm_c_re, ssm_c_im)
    rope_tab = _rope_table(n)
    LAT, CTX = 0, 1

    for l in range(depth):
        with_ctx_out = l < depth - 1
        xl = _ffn(xl, p, l, LAT, 0)
        xc = _ffn(xc, p, l, CTX, 0)
        a_l, s_l, q_l, k_l, v_l = _inproj(xl, p, l, LAT, rope_tab)
        a_c, s_c, q_c, k_c, v_c = _inproj(xc, p, l, CTX)
        y_c, y_l = _ssm(s_c, s_l, sp, l)
        o_l = _attention(q_l, jnp.concatenate([k_l, k_c], axis=0), jnp.concatenate([v_l, v_c], axis=2))
        xl = _merge(xl, a_l, s_l, y_l, o_l, p, l, LAT)
        if with_ctx_out:
            o_c = _attention(q_c, k_c, v_c)
            xc = _merge(xc, a_c, s_c, y_c, o_c, p, l, CTX)
        last = l == depth - 1
        xl = _ffn(xl, p, l, LAT, 1, final_norm_g if last else None)
        if with_ctx_out:
            xc = _ffn(xc, p, l, CTX, 1)
    return xl[None]
```

```python
import functools
import math

import jax
import jax.numpy as jnp
from jax import lax
from jax.experimental import pallas as pl
from jax.experimental.pallas import tpu as pltpu

F32 = jnp.float32
BF16 = jnp.bfloat16

D_MODEL = 1024
D_FF = 2816
GRID_W = 64
POOL_GROUP = 64
POOL_W = 256
SSM_W = 256
SSM_GROUP = 16
SSM_GROUPS = 16
SSM_STATE = 64
HEAD_DIM = 64
N_Q_HEADS = 8
N_KV_HEADS = 2
KV_REP = N_Q_HEADS // N_KV_HEADS
Q_W = N_Q_HEADS * HEAD_DIM
KV_W = N_KV_HEADS * HEAD_DIM
ROPE_THETA = 10000.0
EPS = 1e-6
N_PROJ = POOL_W + SSM_W + Q_W + 2 * KV_W
QK_W = Q_W + KV_W
ROT = HEAD_DIM // 4

LANES = 128
SUBLANES = 8
BF16_ROWS = 16
VMEM_LIMIT_BYTES = 56 * 1024 * 1024

ROW_TILE = 512
SSM_CHUNK = 64
SSM_LH = SSM_CHUNK * SSM_GROUP
POOL_HALO = 8
ATTN_UNROLL = 8
ATTN_STREAM_W = 256
Q_SCALE = HEAD_DIM ** -0.5 * math.log2(math.e)


def _row_tile(n):
    for t in (ROW_TILE, ROW_TILE // 2):
        if n % t == 0:
            return t
    raise ValueError(f"row count {n} must be a multiple of {ROW_TILE // 2}")


def _pick(block, *index):
    return pl.BlockSpec(block, lambda *_: index, pipeline_mode=pl.Buffered(1))


def _params(*sem, **kw):
    return pltpu.CompilerParams(dimension_semantics=sem, vmem_limit_bytes=VMEM_LIMIT_BYTES, **kw)


def _modulate(x, g, shift, scale):
    ms = jnp.mean(x * x, axis=-1, keepdims=True)
    xn = x * lax.rsqrt(ms + EPS) * g
    return xn * (1.0 + scale) + shift


def _split2(x):
    hi = x.astype(BF16)
    lo = (x - hi.astype(F32)).astype(BF16)
    return hi, lo


def _dot(a, b):
    return jnp.dot(a, b, preferred_element_type=F32)


def _dot_sel(xs, sel):
    rows = xs[0].shape[0]
    parts = [part for x in xs for part in _split2(x)]
    out = _dot(jnp.concatenate(parts, axis=0), sel)
    return [out[2 * i * rows:(2 * i + 1) * rows] + out[(2 * i + 1) * rows:(2 * i + 2) * rows]
            for i in range(len(xs))]


def _dot_f32(a, b):
    ah, al = _split2(a)
    bh, bl = _split2(b)
    return _dot(ah, bh) + _dot(ah, bl) + _dot(al, bh)


def _adaln_kernel(c_ref, w_ref, b_ref, o_ref):
    c = c_ref[...]
    cs = c * jax.nn.sigmoid(c)
    o_ref[...] = _dot_f32(cs, w_ref[...]) + b_ref[...]


def _adaln(cond8, mod_w, mod_b):
    depth, d, nm = mod_w.shape
    tn = nm // 4
    return pl.pallas_call(
        _adaln_kernel,
        grid=(depth, nm // tn),
        in_specs=[
            pl.BlockSpec((SUBLANES, d), lambda l, j: (0, 0)),
            pl.BlockSpec((None, d, tn), lambda l, j: (l, 0, j)),
            pl.BlockSpec((None, 1, tn), lambda l, j: (l, 0, j)),
        ],
        out_specs=pl.BlockSpec((None, SUBLANES, tn), lambda l, j: (l, 0, j)),
        out_shape=jax.ShapeDtypeStruct((depth, SUBLANES, nm), F32),
        compiler_params=_params("arbitrary", "arbitrary"),
        name="adaln",
    )(cond8, mod_w, mod_b.reshape(depth, 1, nm))


def _ffn_kernel(x_ref, mod_ref, g_ref, w13_ref, w2_ref, *rest, norm_row, final):
    o_ref = rest[-1]
    x = x_ref[...]
    h = _modulate(x, g_ref[norm_row:norm_row + 1, :], mod_ref[0:1, :], mod_ref[1:2, :]).astype(BF16)
    g = _dot(h, w13_ref[:, :D_FF])
    u = _dot(h, w13_ref[:, D_FF:])
    acc = _dot((g * jax.nn.sigmoid(g) * u).astype(BF16), w2_ref[...])
    y = x + (0.5 * mod_ref[2:3, :]) * acc
    if final:
        fg_ref = rest[0]
        y = y * lax.rsqrt(jnp.mean(y * y, axis=-1, keepdims=True) + EPS) * fg_ref[...]
    o_ref[...] = y


def _ffn(x, p, l, stream, which, final_g=None):
    n, d = x.shape
    tm = _row_tile(n)
    final = final_g is not None
    in_specs = [
        pl.BlockSpec((tm, d), lambda i: (i, 0)),
        _pick((None, None, None, 3, d), l, stream, 2 * which, 0, 0),
        _pick((None, 3, d), l, 0, 0),
        _pick((None, None, d, 2 * D_FF), l, which, 0, 0),
        _pick((None, None, D_FF, d), l, which, 0, 0),
    ]
    args = [x, p["mods"], p["norm_g"], p["w13"], p["w2"]]
    if final:
        in_specs.append(_pick((1, d), 0, 0))
        args.append(final_g.reshape(1, d))
    return pl.pallas_call(
        functools.partial(_ffn_kernel, norm_row=2 * which, final=final),
        grid=(n // tm,),
        in_specs=in_specs,
        out_specs=pl.BlockSpec((tm, d), lambda i: (i, 0)),
        out_shape=jax.ShapeDtypeStruct((n, d), F32),
        compiler_params=_params("arbitrary"),
        name="ffn_final" if final else "ffn",
    )(*args)


def _inproj_kernel(x_ref, mod_ref, g_ref, ws_ref, wt_ref, gn_ref, *rest, rope):
    if rope:
        rope_ref = rest[0]
        rest = rest[1:]
    a_ref, s_ref, q_ref, k_ref, v_ref = rest
    tm = x_ref.shape[0]
    x = x_ref[...]
    h = _modulate(x, g_ref[1:2, :], mod_ref[0:1, :], mod_ref[1:2, :]).astype(BF16)
    ps = _dot(h, ws_ref[...])
    a_ref[...] = ps[:, 0:POOL_W]
    s_ref[...] = ps[:, POOL_W:]
    pt = lax.dot_general(wt_ref[...], h, (((1,), (1,)), ((), ())), preferred_element_type=F32)
    zeros = jnp.zeros((HEAD_DIM, tm), BF16)
    k_heads = []
    for hd in range(N_Q_HEADS + N_KV_HEADS):
        blk = pt[hd * HEAD_DIM:(hd + 1) * HEAD_DIM, :]
        ss = jnp.sum(blk * blk, axis=0, keepdims=True)
        xn = blk * lax.rsqrt(ss * (1.0 / HEAD_DIM) + EPS) * gn_ref[0 if hd < N_Q_HEADS else 1]
        if rope:
            x0, x1, x2, x3 = (xn[i * ROT:(i + 1) * ROT, :] for i in range(4))
            ca, sa, cb, sb = (rope_ref[i * ROT:(i + 1) * ROT, :] for i in range(4))
            xn = jnp.concatenate([x0 * ca - x1 * sa, x1 * ca + x0 * sa, x2 * cb - x3 * sb, x3 * cb + x2 * sb], axis=0)
        if hd < N_Q_HEADS:
            qb = xn.astype(BF16)
            q_ref[hd] = jnp.concatenate([qb, zeros] if hd < KV_REP else [zeros, qb], axis=0)
        else:
            k_heads.append(xn)
    k_ref[...] = jnp.concatenate(k_heads, axis=0).T.astype(BF16)
    v_ref[...] = pt[QK_W:, :].reshape(N_KV_HEADS, HEAD_DIM, tm).astype(BF16)


def _inproj(x, p, l, stream, rope_tab=None):
    n, d = x.shape
    tm = _row_tile(n)
    rope = rope_tab is not None
    in_specs = [
        pl.BlockSpec((tm, d), lambda i: (i, 0)),
        _pick((None, None, None, 3, d), l, stream, 1, 0, 0),
        _pick((None, 3, d), l, 0, 0),
        _pick((None, d, POOL_W + SSM_W), l, 0, 0),
        _pick((None, QK_W + KV_W, d), l, 0, 0),
        _pick((None, 2, HEAD_DIM, tm), l, 0, 0, 0),
    ]
    args = [x, p["mods"], p["norm_g"], p["w_s"], p["w_t"], p["gn"]]
    if rope:
        in_specs.append(pl.BlockSpec((HEAD_DIM, tm), lambda i: (0, i)))
        args.append(rope_tab)
    return pl.pallas_call(
        functools.partial(_inproj_kernel, rope=rope),
        grid=(n // tm,),
        in_specs=in_specs,
        out_specs=[
            pl.BlockSpec((tm, POOL_W), lambda i: (i, 0)),
            pl.BlockSpec((tm, SSM_W), lambda i: (i, 0)),
            pl.BlockSpec((None, N_Q_HEADS, KV_W, tm), lambda i: (i, 0, 0, 0)),
            pl.BlockSpec((tm, KV_W), lambda i: (i, 0)),
            pl.BlockSpec((N_KV_HEADS, HEAD_DIM, tm), lambda i: (0, 0, i)),
        ],
        out_shape=[
            jax.ShapeDtypeStruct((n, POOL_W), F32),
            jax.ShapeDtypeStruct((n, SSM_W), F32),
            jax.ShapeDtypeStruct((n // tm, N_Q_HEADS, KV_W, tm), BF16),
            jax.ShapeDtypeStruct((n, KV_W), BF16),
            jax.ShapeDtypeStruct((N_KV_HEADS, HEAD_DIM, n), BF16),
        ],
        compiler_params=_params("arbitrary"),
        name="inproj_rope" if rope else "inproj",
    )(*args)


def _rope_table(n):
    t = jnp.arange(n)
    rows = (t // GRID_W).astype(F32)
    cols = (t % GRID_W).astype(F32)
    half = HEAD_DIM // 2
    inv = ROPE_THETA ** (-jnp.arange(0, half, 2, dtype=F32) / half)
    ang_r = inv[:, None] * rows[None, :]
    ang_c = inv[:, None] * cols[None, :]
    return jnp.concatenate([jnp.cos(ang_r), jnp.sin(ang_r), jnp.cos(ang_c), jnp.sin(ang_c)], axis=0)


def _ssm_kernel(ul_ref, uc_ref, pc_ref, pr_ref, bc_ref, bt_ref, ct_ref, yl_ref, yc_ref,
                t_ref, w1_ref, w3_ref, sl_ref, sin_ref, *, n_ctx_chunks):
    L, H, P, LH = SSM_CHUNK, SSM_GROUP, SSM_STATE, SSM_LH
    ncl = ul_ref.shape[0]
    ctx_rows = uc_ref.shape[0]
    pc = pc_ref[...]
    pr = pr_ref[...]

    def powers(o):
        dt = jnp.exp(pc[:, o + 2:o + 3])
        zr = pc[:, o:o + 1] * dt
        zi = pc[:, o + 1:o + 2] * dt
        m = lax.broadcasted_iota(jnp.int32, (1, LANES), 1).astype(F32)
        mag = jnp.exp(zr * m)
        return mag * jnp.cos(zi * m), mag * jnp.sin(zi * m)

    def zoh(lr, li, ldt):
        dt = jnp.exp(ldt)
        mag = jnp.exp(lr * dt)
        nr = mag * jnp.cos(li * dt) - 1.0
        ni = mag * jnp.sin(li * dt)
        den = lr * lr + li * li
        return (nr * lr + ni * li) / den, (ni * lr - nr * li) / den

    def sel_mat(expo):
        w = expo.shape[1]
        return (lax.broadcasted_iota(jnp.int32, (LANES, w), 0) == expo).astype(BF16)

    def lane_tile(x, w):
        return jnp.concatenate([x] * (w // LANES), axis=1)

    def cmul(ar, ai, br, bi):
        return ar * br - ai * bi, ar * bi + ai * br

    tok_lh = lax.broadcasted_iota(jnp.int32, (1, LH), 1) // H
    lag2 = lax.broadcasted_iota(jnp.int32, (1, 2 * LH), 1) // H - (L - 1)

    w1_ref[...] = jnp.zeros_like(w1_ref)
    w3_ref[...] = jnp.zeros_like(w3_ref)
    klong = None
    a_chunk = []
    for dr in range(2):
        o = 3 * dr
        apr, api = powers(o)
        qr, qi = zoh(pr[o:o + 1, :], pr[o + 1:o + 2, :], pr[o + 2:o + 3, :])
        bbr_t, bbi_t = cmul(qr, qi, bt_ref[2 * dr], bt_ref[2 * dr + 1])
        lag = lag2 if dr == 0 else -lag2
        sel = sel_mat(lag)
        er, ei = _dot_sel([apr, api], sel)
        gr, gi = cmul(er, ei, lane_tile(ct_ref[2 * dr], 2 * LH), lane_tile(ct_ref[2 * dr + 1], 2 * LH))
        kd = _dot_f32(bbr_t, gr) - _dot_f32(bbi_t, gi)
        klong = kd if klong is None else klong + kd
        qrc, qic = zoh(pc[:, o:o + 1], pc[:, o + 1:o + 2], pc[:, o + 2:o + 3])
        bbr, bbi = cmul(qrc, qic, bc_ref[2 * dr], bc_ref[2 * dr + 1])
        sel = sel_mat((L - 1 - tok_lh) if dr == 0 else tok_lh)
        er, ei = _dot_sel([apr, api], sel)
        wr, wi = cmul(er, ei, lane_tile(bbr, LH), lane_tile(bbi, LH))
        w1_ref[(4 * dr) * P:(4 * dr + 1) * P, :] = wr.astype(BF16)
        w1_ref[(4 * dr + 2) * P:(4 * dr + 3) * P, :] = wi.astype(BF16)
        sel = sel_mat((tok_lh + 1) if dr == 0 else (L - tok_lh))
        er, ei = _dot_sel([apr, api], sel)
        gr, gi = cmul(er, ei, lane_tile(ct_ref[2 * dr], LH), lane_tile(ct_ref[2 * dr + 1], LH))
        w3_ref[(4 * dr) * P:(4 * dr + 1) * P, :] = gr.astype(BF16)
        w3_ref[(4 * dr + 2) * P:(4 * dr + 3) * P, :] = (-gi).astype(BF16)
        dtr = jnp.exp(pr[o + 2:o + 3, :])
        mag = jnp.exp(pr[o:o + 1, :] * dtr * L)
        ang = pr[o + 1:o + 2, :] * dtr * L
        pad = jnp.zeros((1, LANES - P), F32)
        a_chunk.append((jnp.concatenate([mag * jnp.cos(ang), pad], axis=1),
                        jnp.concatenate([mag * jnp.sin(ang), pad], axis=1)))

    for j in range(L):
        start = (L - 1 - j) * H
        t_ref[j * H:(j + 1) * H, :] = klong[:, start:start + LH].astype(BF16)

    nt = (((1,), (1,)), ((), ()))
    ul = ul_ref[...]
    uc = uc_ref[...]
    sl_ref[0:ncl, :] = lax.dot_general(ul, w1_ref[...], nt, preferred_element_type=F32)
    sl_ref[ncl:, :] = lax.dot_general(uc, w1_ref[...], nt, preferred_element_type=F32)
    sin_ref[ncl + SUBLANES:, :] = jnp.zeros((ctx_rows - SUBLANES, 4 * LANES), F32)

    (afr, afi), (abr, abi) = a_chunk
    rid = lax.broadcasted_iota(jnp.int32, (SUBLANES, LANES), 0)

    def block_scan(carry, rf0, rb0, n_rows):
        fr, fi, br, bi = carry
        lf_r, lf_i = sl_ref[pl.ds(rf0, SUBLANES), 0:LANES], sl_ref[pl.ds(rf0, SUBLANES), LANES:2 * LANES]
        lb_r = sl_ref[pl.ds(rb0, SUBLANES), 2 * LANES:3 * LANES]
        lb_i = sl_ref[pl.ds(rb0, SUBLANES), 3 * LANES:4 * LANES]
        of_r = of_i = ob_r = ob_i = jnp.zeros((SUBLANES, LANES), F32)
        for k in range(n_rows):
            r = k
            of_r = jnp.where(rid == r, fr, of_r)
            of_i = jnp.where(rid == r, fi, of_i)
            fr, fi = (afr * fr - afi * fi + lf_r[r:r + 1, :], afr * fi + afi * fr + lf_i[r:r + 1, :])
            r = n_rows - 1 - k
            ob_r = jnp.where(rid == r, br, ob_r)
            ob_i = jnp.where(rid == r, bi, ob_i)
            br, bi = (abr * br - abi * bi + lb_r[r:r + 1, :], abr * bi + abi * br + lb_i[r:r + 1, :])
        sin_ref[pl.ds(rf0, SUBLANES), 0:LANES] = of_r
        sin_ref[pl.ds(rf0, SUBLANES), LANES:2 * LANES] = of_i
        sin_ref[pl.ds(rb0, SUBLANES), 2 * LANES:3 * LANES] = ob_r
        sin_ref[pl.ds(rb0, SUBLANES), 3 * LANES:4 * LANES] = ob_i
        return fr, fi, br, bi

    z = jnp.zeros((1, LANES), F32)
    carry = block_scan((z, z, z, z), ncl, ncl, n_ctx_chunks)
    nblk = ncl // SUBLANES

    def step(b, carry):
        rf0 = pl.multiple_of(b * SUBLANES, SUBLANES)
        rb0 = pl.multiple_of((nblk - 1 - b) * SUBLANES, SUBLANES)
        return block_scan(carry, rf0, rb0, SUBLANES)

    lax.fori_loop(0, nblk, step, carry)

    yl_ref[...] = (_dot(ul, t_ref[...]) + _dot(sin_ref[0:ncl, :].astype(BF16), w3_ref[...])).astype(BF16)
    yc_ref[...] = (_dot(uc, t_ref[...]) + _dot(sin_ref[ncl:, :].astype(BF16), w3_ref[...])).astype(BF16)


def _ssm_params(a_re, a_im, log_dt, b_re, b_im, c_re, c_im):
    depth, _, G, P = a_re.shape
    ldt = jnp.broadcast_to(log_dt[:, :, :, None], (depth, 2, G, P))
    zero = jnp.zeros((depth, G, P), F32)
    rows = jnp.stack([a_re[:, 0], a_im[:, 0], ldt[:, 0], a_re[:, 1], a_im[:, 1], ldt[:, 1], zero, zero],
                     axis=2)
    b_col = jnp.stack([b_re[:, 0], b_im[:, 0], b_re[:, 1], b_im[:, 1]], axis=2)
    c_t = jnp.stack([c_re[:, 0], c_im[:, 0], c_re[:, 1], c_im[:, 1]], axis=2).transpose(0, 1, 2, 4, 3)
    rep = (1, 1, 1, 1, LANES // b_col.shape[-1])
    return dict(cols=rows.transpose(0, 1, 3, 2), rows=rows, b_col=jnp.tile(b_col, rep),
                b_t=b_col.transpose(0, 1, 2, 4, 3), c_t=jnp.tile(c_t, rep))


def _ssm(s_ctx, s_lat, sp, l):
    G, H, P, L, LH = SSM_GROUPS, SSM_GROUP, SSM_STATE, SSM_CHUNK, SSM_LH
    n_ctx, n_lat = s_ctx.shape[0], s_lat.shape[0]
    assert n_ctx % L == 0 and n_ctx // L <= SUBLANES and n_lat % (L * BF16_ROWS) == 0
    ncl, ncc = n_lat // L, n_ctx // L

    def chunked(s):
        return s.astype(BF16).reshape(-1, L, G, H).transpose(2, 0, 1, 3).reshape(G, -1, LH)

    ul = chunked(s_lat)
    uc = jnp.pad(chunked(s_ctx), ((0, 0), (0, BF16_ROWS - ncc), (0, 0)))

    def gspec(shape):
        nd = len(shape)
        return pl.BlockSpec((None,) + shape, lambda g: (g,) + (0,) * nd)

    def lspec(shape):
        nd = len(shape)
        return pl.BlockSpec((None, None) + shape, lambda g: (l, g) + (0,) * nd)

    yl, yc = pl.pallas_call(
        functools.partial(_ssm_kernel, n_ctx_chunks=ncc),
        grid=(G,),
        in_specs=[gspec((ncl, LH)), gspec((BF16_ROWS, LH)), lspec((P, 8)), lspec((8, P)), lspec((4, P, LANES)),
                  lspec((4, H, P)), lspec((4, P, LANES))],
        out_specs=[gspec((ncl, LH)), gspec((BF16_ROWS, LH))],
        out_shape=[jax.ShapeDtypeStruct((G, ncl, LH), BF16), jax.ShapeDtypeStruct((G, BF16_ROWS, LH), BF16)],
        scratch_shapes=[
            pltpu.VMEM((LH, LH), BF16),
            pltpu.VMEM((8 * P, LH), BF16),
            pltpu.VMEM((8 * P, LH), BF16),
            pltpu.VMEM((ncl + BF16_ROWS, 4 * LANES), F32),
            pltpu.VMEM((ncl + BF16_ROWS, 4 * LANES), F32),
        ],
        compiler_params=_params("arbitrary"),
        name="ssm",
    )(ul, uc, sp["cols"], sp["rows"], sp["b_col"], sp["b_t"], sp["c_t"])

    def unchunked(y):
        return y.reshape(G, -1, L, H).transpose(1, 2, 0, 3).reshape(-1, G * H)

    return unchunked(yc[:, :ncc]), unchunked(yl)


def _attn_kernel(q_ref, k_ref, v_ref, o_ref, s_ref, m_ref, l_ref, acc_ref):
    nch = k_ref.shape[0]
    tq = q_ref.shape[2]
    width = min(ATTN_STREAM_W, tq)
    streams = [(h, slice(u * width, (u + 1) * width)) for h in range(KV_REP) for u in range(tq // width)]
    m_ref[...] = jnp.full_like(m_ref, -jnp.inf)
    l_ref[...] = jnp.zeros_like(l_ref)
    acc_ref[...] = jnp.zeros_like(acc_ref)

    def scores(c, slot, st):
        h, lanes = st
        s = _dot(k_ref[c], q_ref[h, :, lanes])
        s_ref[slot, h, :, lanes] = s
        return jnp.max(s, axis=0, keepdims=True)

    def consume(c, slot, st, mx):
        h, lanes = st
        m_old = m_ref[h, :, lanes]
        m_new = jnp.maximum(m_old, mx)
        alpha = jnp.exp2(m_old - m_new)
        p = jnp.exp2(s_ref[slot, h, :, lanes] - m_new)
        l_ref[h, :, lanes] = alpha * l_ref[h, :, lanes] + jnp.sum(p, axis=0, keepdims=True)
        acc_ref[h, :, lanes] = alpha * acc_ref[h, :, lanes] + _dot(v_ref[c], p.astype(BF16))
        m_ref[h, :, lanes] = m_new

    def half(c, slot, mx):
        new = []
        for i, st in enumerate(streams):
            new.append(scores(c + 1, 1 - slot, st))
            consume(c, slot, st, mx[i])
        return tuple(new)

    def body(d, mx):
        for u in range(ATTN_UNROLL):
            mx = half(ATTN_UNROLL * d + u, u % 2, mx)
        return mx

    mx = tuple(scores(0, 0, st) for st in streams)
    nloop = (nch - 1) // ATTN_UNROLL
    mx = lax.fori_loop(0, nloop, body, mx)
    c = ATTN_UNROLL * nloop
    while c < nch - 1:
        mx = half(c, c % 2, mx)
        c += 1
    for i, st in enumerate(streams):
        consume(c, c % 2, st, mx[i])
    for h in range(KV_REP):
        o_ref[h] = (acc_ref[h] / l_ref[h]).astype(BF16)


def _key_chunk(nk):
    for tk in (640, 512, 256):
        if nk % tk == 0:
            return tk
    raise ValueError(f"key count {nk} must be a multiple of 256")


def _attention(qp, k, vt):
    nt, _, _, tq = qp.shape
    nk = k.shape[0]
    tk = _key_chunk(nk)
    nch = nk // tk
    kr = k.reshape(nch, tk, KV_W)
    vr = vt.reshape(N_KV_HEADS, HEAD_DIM, nch, tk).transpose(0, 2, 1, 3)
    return pl.pallas_call(
        _attn_kernel,
        grid=(N_KV_HEADS, nt),
        in_specs=[
            pl.BlockSpec((None, KV_REP, KV_W, tq), lambda j, i: (i, j, 0, 0)),
            pl.BlockSpec((nch, tk, KV_W), lambda j, i: (0, 0, 0)),
            pl.BlockSpec((None, nch, HEAD_DIM, tk), lambda j, i: (j, 0, 0, 0)),
        ],
        out_specs=pl.BlockSpec((None, KV_REP, HEAD_DIM, tq), lambda j, i: (i, j, 0, 0)),
        out_shape=jax.ShapeDtypeStruct((nt, N_Q_HEADS, HEAD_DIM, tq), BF16),
        scratch_shapes=[
            pltpu.VMEM((2, KV_REP, tk, tq), F32),
            pltpu.VMEM((KV_REP, 1, tq), F32),
            pltpu.VMEM((KV_REP, 1, tq), F32),
            pltpu.VMEM((KV_REP, HEAD_DIM, tq), F32),
        ],
        compiler_params=_params("arbitrary", "arbitrary"),
        name="attention",
    )(qp, kr, vr)


def _merge_kernel(x_ref, a_ref, ap_ref, an_ref, s_ref, y_ref, o_ref, mod_ref, g_ref, wg_ref, pw_ref, ps_ref,
                  po_ref, d_ref, wglu_ref, wao_ref, wout_ref, out_ref, ext_ref, *, n_rows):
    tm = x_ref.shape[0]
    i = pl.program_id(0)
    x = x_ref[...]
    h = _modulate(x, g_ref[1:2, :], mod_ref[0:1, :], mod_ref[1:2, :]).astype(BF16)
    gates = jax.nn.sigmoid(_dot(h, wg_ref[...]))

    a = a_ref[...]
    ext_ref[0:POOL_HALO, :] = jnp.where(i > 0, ap_ref[...], 0.0)
    ext_ref[POOL_HALO:POOL_HALO + tm, :] = a
    ext_ref[POOL_HALO + tm:, :] = jnp.where(i < pl.num_programs(0) - 1, an_ref[...], 0.0)
    e = ext_ref[...]
    r = tm + 2 * POOL_HALO
    s2 = e + pltpu.roll(e, 1, 0)
    s4 = pltpu.roll(s2, 1, 0) + pltpu.roll(s2, r - 1, 0)
    s8 = pltpu.roll(s4, 2, 0) + pltpu.roll(s4, r - 2, 0)
    s16 = pltpu.roll(s8, 4, 0) + pltpu.roll(s8, r - 4, 0)
    lane_grp = lax.broadcasted_iota(jnp.int32, (1, POOL_W), 1) // POOL_GROUP
    sums = jnp.where(lane_grp == 0, s2, jnp.where(lane_grp == 1, s4, jnp.where(lane_grp == 2, s8, s16)))
    sums = sums[POOL_HALO:POOL_HALO + tm, :]
    half_w = jnp.left_shift(1, lane_grp)
    row = i * tm + lax.broadcasted_iota(jnp.int32, (tm, 1), 0)
    cnt = jnp.minimum(row + half_w, n_rows) - jnp.maximum(row - half_w, 0)
    pooled = sums / cnt.astype(F32) - a
    ya = _dot(pooled.astype(BF16), pw_ref[...]) * ps_ref[...]
    br_a = _dot(ya.astype(BF16), po_ref[...])

    y = s_ref[...] * d_ref[...] + y_ref[...].astype(F32)
    z = _dot(jax.nn.gelu(y).astype(BF16), wglu_ref[...])
    br_b = z[:, :D_MODEL] * jax.nn.sigmoid(z[:, D_MODEL:])

    o_t = o_ref[...].reshape(Q_W, tm)
    br_c = lax.dot_general(o_t, wao_ref[...], (((0,), (0,)), ((), ())), preferred_element_type=F32)

    mixed = (gates[:, :D_MODEL] * br_a + gates[:, D_MODEL:2 * D_MODEL] * br_b
             + gates[:, 2 * D_MODEL:] * br_c)
    out_ref[...] = x + mod_ref[2:3, :] * _dot(mixed.astype(BF16), wout_ref[...])


def _merge(x, a, s, y_ssm, o_attn, p, l, stream):
    n, d = x.shape
    tm = _row_tile(n)
    nb = tm // POOL_HALO
    last = n // POOL_HALO - 1
    row = lambda w: pl.BlockSpec((tm, w), lambda i: (i, 0))
    return pl.pallas_call(
        functools.partial(_merge_kernel, n_rows=n),
        grid=(n // tm,),
        in_specs=[
            row(d), row(POOL_W),
            pl.BlockSpec((POOL_HALO, POOL_W), lambda i: (jnp.maximum(i * nb - 1, 0), 0)),
            pl.BlockSpec((POOL_HALO, POOL_W), lambda i: (jnp.minimum((i + 1) * nb, last), 0)),
            row(SSM_W), row(SSM_W),
            pl.BlockSpec((None, N_Q_HEADS, HEAD_DIM, tm), lambda i: (i, 0, 0, 0)),
            _pick((None, None, None, 3, d), l, stream, 1, 0, 0),
            _pick((None, 3, d), l, 0, 0),
            _pick((None, d, 3 * d), l, 0, 0),
            _pick((None, POOL_W, POOL_W), l, 0, 0),
            _pick((None, 1, POOL_W), l, 0, 0),
            _pick((None, POOL_W, d), l, 0, 0),
            _pick((None, 1, SSM_W), l, 0, 0),
            _pick((None, SSM_W, 2 * d), l, 0, 0),
            _pick((None, Q_W, d), l, 0, 0),
            _pick((None, d, d), l, 0, 0),
        ],
        out_specs=row(d),
        out_shape=jax.ShapeDtypeStruct((n, d), F32),
        scratch_shapes=[pltpu.VMEM((tm + 2 * POOL_HALO, POOL_W), F32)],
        compiler_params=_params("arbitrary"),
        name="merge",
    )(x, a, a, a, s, y_ssm, o_attn, p["mods"], p["norm_g"], p["w_g"], p["pool_bd"], p["pool_scale"],
      p["pool_out"], p["ssm_d"], p["w_glu"], p["w_ao"], p["w_out"])


def kernel(x, c, ctx, c_ctx, mod_w, mod_b, norm_g, ffn_w13, ffn_w2, w_in, pool_w, pool_scale, pool_out, ssm_a_re, ssm_a_im, ssm_log_dt, ssm_b_re, ssm_b_im, ssm_c_re, ssm_c_im, ssm_d, ssm_glu_w, q_norm_g, k_norm_g, attn_out, w_out, final_norm_g):
    batch, n, d = x.shape
    depth = mod_w.shape[0]
    assert batch == 1 and d == D_MODEL and n % GRID_W == 0
    xl = x[0]
    xc = ctx[0]

    cond8 = jnp.concatenate([c, c_ctx[None], jnp.zeros((SUBLANES - 2, d), F32)], axis=0)
    n_groups = POOL_W // POOL_GROUP
    pool_bd = (jnp.eye(n_groups, dtype=F32)[None, :, None, :, None] * pool_w[:, :, :, None, :]
               ).reshape(depth, POOL_W, POOL_W)
    gn = jnp.stack([q_norm_g * Q_SCALE, k_norm_g], axis=1)
    p = dict(
        mods=_adaln(cond8, mod_w, mod_b).reshape(depth, SUBLANES, 3, 3, d),
        norm_g=norm_g,
        w13=ffn_w13.astype(BF16),
        w2=ffn_w2.astype(BF16),
        w_s=w_in[:, :, :POOL_W + SSM_W].astype(BF16),
        w_t=w_in[:, :, POOL_W + SSM_W:N_PROJ].transpose(0, 2, 1).astype(BF16),
        w_g=w_in[:, :, N_PROJ:].astype(BF16),
        gn=jnp.broadcast_to(gn[:, :, :, None], (depth, 2, HEAD_DIM, ROW_TILE)),
        pool_bd=pool_bd.astype(BF16),
        pool_scale=pool_scale.reshape(depth, 1, POOL_W),
        pool_out=pool_out.astype(BF16),
        ssm_d=ssm_d.reshape(depth, 1, SSM_W),
        w_glu=ssm_glu_w.astype(BF16),
        w_ao=attn_out.astype(BF16),
        w_out=w_out.astype(BF16),
    )
    sp = _ssm_params(ssm_a_re, ssm_a_im, ssm_log_dt, ssm_b_re, ssm_b_im, ssm_c_re, ssm_c_im)
    rope_tab = _rope_table(n)
    LAT, CTX = 0, 1

    for l in range(depth):
        with_ctx_out = l < depth - 1
        xl = _ffn(xl, p, l, LAT, 0)
        xc = _ffn(xc, p, l, CTX, 0)
        a_l, s_l, q_l, k_l, v_l = _inproj(xl, p, l, LAT, rope_tab)
        a_c, s_c, q_c, k_c, v_c = _inproj(xc, p, l, CTX)
        y_c, y_l = _ssm(s_c, s_l, sp, l)
        o_l = _attention(q_l, jnp.concatenate([k_l, k_c], axis=0), jnp.concatenate([v_l, v_c], axis=2))
        xl = _merge(xl, a_l, s_l, y_l, o_l, p, l, LAT)
        if with_ctx_out:
            o_c = _attention(q_c, k_c, v_c)
            xc = _merge(xc, a_c, s_c, y_c, o_c, p, l, CTX)
        last = l == depth - 1
        xl = _ffn(xl, p, l, LAT, 1, final_norm_g if last else None)
        if with_ctx_out:
            xc = _ffn(xc, p, l, CTX, 1)
    return xl[None]
```

```python
import functools
import math

import jax
import jax.numpy as jnp
from jax import lax
from jax.experimental import pallas as pl
from jax.experimental.pallas import tpu as pltpu

F32 = jnp.float32
BF16 = jnp.bfloat16

D_MODEL = 1024
D_FF = 2816
GRID_W = 64
POOL_GROUP = 64
POOL_W = 256
SSM_W = 256
SSM_GROUP = 16
SSM_GROUPS = 16
SSM_STATE = 64
HEAD_DIM = 64
N_Q_HEADS = 8
N_KV_HEADS = 2
KV_REP = N_Q_HEADS // N_KV_HEADS
Q_W = N_Q_HEADS * HEAD_DIM
KV_W = N_KV_HEADS * HEAD_DIM
ROPE_THETA = 10000.0
EPS = 1e-6
N_PROJ = POOL_W + SSM_W + Q_W + 2 * KV_W
QK_W = Q_W + KV_W
ROT = HEAD_DIM // 4

LANES = 128
SUBLANES = 8
BF16_ROWS = 16
VMEM_LIMIT_BYTES = 56 * 1024 * 1024

ROW_TILE = 512
SSM_CHUNK = 64
SSM_LH = SSM_CHUNK * SSM_GROUP
POOL_HALO = 8
ATTN_UNROLL = 8
ATTN_STREAM_W = 256
Q_SCALE = HEAD_DIM ** -0.5 * math.log2(math.e)


def _row_tile(n):
    for t in (ROW_TILE, ROW_TILE // 2):
        if n % t == 0:
            return t
    raise ValueError(f"row count {n} must be a multiple of {ROW_TILE // 2}")


def _pick(block, *index):
    return pl.BlockSpec(block, lambda *_: index, pipeline_mode=pl.Buffered(1))


def _params(*sem, **kw):
    return pltpu.CompilerParams(dimension_semantics=sem, vmem_limit_bytes=VMEM_LIMIT_BYTES, **kw)


def _modulate(x, g, shift, scale):
    ms = jnp.mean(x * x, axis=-1, keepdims=True)
    xn = x * lax.rsqrt(ms + EPS) * g
    return xn * (1.0 + scale) + shift


def _split2(x):
    hi = x.astype(BF16)
    lo = (x - hi.astype(F32)).astype(BF16)
    return hi, lo


def _dot(a, b):
    return jnp.dot(a, b, preferred_element_type=F32)


def _dot_sel(xs, sel):
    rows = xs[0].shape[0]
    parts = [part for x in xs for part in _split2(x)]
    out = _dot(jnp.concatenate(parts, axis=0), sel)
    return [out[2 * i * rows:(2 * i + 1) * rows] + out[(2 * i + 1) * rows:(2 * i + 2) * rows]
            for i in range(len(xs))]


def _dot_f32(a, b):
    ah, al = _split2(a)
    bh, bl = _split2(b)
    return _dot(ah, bh) + _dot(ah, bl) + _dot(al, bh)


def _adaln_kernel(c_ref, w_ref, b_ref, o_ref):
    c = c_ref[...]
    cs = c * jax.nn.sigmoid(c)
    o_ref[...] = _dot_f32(cs, w_ref[...]) + b_ref[...]


def _adaln(cond8, mod_w, mod_b):
    depth, d, nm = mod_w.shape
    tn = nm // 4
    return pl.pallas_call(
        _adaln_kernel,
        grid=(depth, nm // tn),
        in_specs=[
            pl.BlockSpec((SUBLANES, d), lambda l, j: (0, 0)),
            pl.BlockSpec((None, d, tn), lambda l, j: (l, 0, j)),
            pl.BlockSpec((None, 1, tn), lambda l, j: (l, 0, j)),
        ],
        out_specs=pl.BlockSpec((None, SUBLANES, tn), lambda l, j: (l, 0, j)),
        out_shape=jax.ShapeDtypeStruct((depth, SUBLANES, nm), F32),
        compiler_params=_params("arbitrary", "arbitrary"),
        name="adaln",
    )(cond8, mod_w, mod_b.reshape(depth, 1, nm))


def _ffn_kernel(x_ref, mod_ref, g_ref, w13_ref, w2_ref, *rest, norm_row, final):
    o_ref = rest[-1]
    x = x_ref[...]
    h = _modulate(x, g_ref[norm_row:norm_row + 1, :], mod_ref[0:1, :], mod_ref[1:2, :]).astype(BF16)
    g = _dot(h, w13_ref[:, :D_FF])
    u = _dot(h, w13_ref[:, D_FF:])
    acc = _dot((g * jax.nn.sigmoid(g) * u).astype(BF16), w2_ref[...])
    y = x + (0.5 * mod_ref[2:3, :]) * acc
    if final:
        fg_ref = rest[0]
        y = y * lax.rsqrt(jnp.mean(y * y, axis=-1, keepdims=True) + EPS) * fg_ref[...]
    o_ref[...] = y


def _ffn(x, p, l, stream, which, final_g=None):
    n, d = x.shape
    tm = _row_tile(n)
    final = final_g is not None
    in_specs = [
        pl.BlockSpec((tm, d), lambda i: (i, 0)),
        _pick((None, None, None, 3, d), l, stream, 2 * which, 0, 0),
        _pick((None, 3, d), l, 0, 0),
        _pick((None, None, d, 2 * D_FF), l, which, 0, 0),
        _pick((None, None, D_FF, d), l, which, 0, 0),
    ]
    args = [x, p["mods"], p["norm_g"], p["w13"], p["w2"]]
    if final:
        in_specs.append(_pick((1, d), 0, 0))
        args.append(final_g.reshape(1, d))
    return pl.pallas_call(
        functools.partial(_ffn_kernel, norm_row=2 * which, final=final),
        grid=(n // tm,),
        in_specs=in_specs,
        out_specs=pl.BlockSpec((tm, d), lambda i: (i, 0)),
        out_shape=jax.ShapeDtypeStruct((n, d), F32),
        compiler_params=_params("arbitrary"),
        name="ffn_final" if final else "ffn",
    )(*args)


def _inproj_kernel(x_ref, mod_ref, g_ref, ws_ref, wt_ref, gn_ref, *rest, rope):
    if rope:
        rope_ref = rest[0]
        rest = rest[1:]
    a_ref, s_ref, sb_ref, q_ref, k_ref, v_ref = rest
    tm = x_ref.shape[0]
    x = x_ref[...]
    h = _modulate(x, g_ref[1:2, :], mod_ref[0:1, :], mod_ref[1:2, :]).astype(BF16)
    ps = _dot(h, ws_ref[:, :POOL_W + SSM_W])
    a_ref[...] = ps[:, 0:POOL_W]
    s_ref[...] = ps[:, POOL_W:]
    sb_ref[...] = ps[:, POOL_W:].astype(BF16)
    pt = lax.dot_general(wt_ref[...], h, (((1,), (1,)), ((), ())), preferred_element_type=F32)
    zeros = jnp.zeros((HEAD_DIM, tm), BF16)
    k_heads = []
    for hd in range(N_Q_HEADS + N_KV_HEADS):
        blk = pt[hd * HEAD_DIM:(hd + 1) * HEAD_DIM, :]
        ss = jnp.sum(blk * blk, axis=0, keepdims=True)
        xn = blk * lax.rsqrt(ss * (1.0 / HEAD_DIM) + EPS) * gn_ref[0 if hd < N_Q_HEADS else 1]
        if rope:
            x0, x1, x2, x3 = (xn[i * ROT:(i + 1) * ROT, :] for i in range(4))
            ca, sa, cb, sb = (rope_ref[i * ROT:(i + 1) * ROT, :] for i in range(4))
            xn = jnp.concatenate([x0 * ca - x1 * sa, x1 * ca + x0 * sa, x2 * cb - x3 * sb, x3 * cb + x2 * sb], axis=0)
        if hd < N_Q_HEADS:
            qb = xn.astype(BF16)
            q_ref[hd] = jnp.concatenate([qb, zeros] if hd < KV_REP else [zeros, qb], axis=0)
        else:
            k_heads.append(xn)
    k_ref[...] = jnp.concatenate(k_heads, axis=0).T.astype(BF16)
    v_ref[...] = pt[QK_W:, :].reshape(N_KV_HEADS, HEAD_DIM, tm).astype(BF16)


def _inproj(x, p, l, stream, rope_tab=None):
    n, d = x.shape
    tm = _row_tile(n)
    rope = rope_tab is not None
    in_specs = [
        pl.BlockSpec((tm, d), lambda i: (i, 0)),
        _pick((None, None, None, 3, d), l, stream, 1, 0, 0),
        _pick((None, 3, d), l, 0, 0),
        _pick((None,) + p["w_in"].shape[1:], l, 0, 0),
        _pick((None, QK_W + KV_W, d), l, 0, 0),
        _pick((None, 2, HEAD_DIM, tm), l, 0, 0, 0),
    ]
    args = [x, p["mods"], p["norm_g"], p["w_in"], p["w_t"], p["gn"]]
    if rope:
        in_specs.append(pl.BlockSpec((HEAD_DIM, tm), lambda i: (0, i)))
        args.append(rope_tab)
    return pl.pallas_call(
        functools.partial(_inproj_kernel, rope=rope),
        grid=(n // tm,),
        in_specs=in_specs,
        out_specs=[
            pl.BlockSpec((tm, POOL_W), lambda i: (i, 0)),
            pl.BlockSpec((tm, SSM_W), lambda i: (i, 0)),
            pl.BlockSpec((tm, SSM_W), lambda i: (i, 0)),
            pl.BlockSpec((None, N_Q_HEADS, KV_W, tm), lambda i: (i, 0, 0, 0)),
            pl.BlockSpec((tm, KV_W), lambda i: (i, 0)),
            pl.BlockSpec((N_KV_HEADS, HEAD_DIM, tm), lambda i: (0, 0, i)),
        ],
        out_shape=[
            jax.ShapeDtypeStruct((n, POOL_W), F32),
            jax.ShapeDtypeStruct((n, SSM_W), F32),
            jax.ShapeDtypeStruct((n, SSM_W), BF16),
            jax.ShapeDtypeStruct((n // tm, N_Q_HEADS, KV_W, tm), BF16),
            jax.ShapeDtypeStruct((n, KV_W), BF16),
            jax.ShapeDtypeStruct((N_KV_HEADS, HEAD_DIM, n), BF16),
        ],
        compiler_params=_params("arbitrary"),
        name="inproj_rope" if rope else "inproj",
    )(*args)


def _rope_table(n):
    t = jnp.arange(n)
    rows = (t // GRID_W).astype(F32)
    cols = (t % GRID_W).astype(F32)
    half = HEAD_DIM // 2
    inv = ROPE_THETA ** (-jnp.arange(0, half, 2, dtype=F32) / half)
    ang_r = inv[:, None] * rows[None, :]
    ang_c = inv[:, None] * cols[None, :]
    return jnp.concatenate([jnp.cos(ang_r), jnp.sin(ang_r), jnp.cos(ang_c), jnp.sin(ang_c)], axis=0)


def _ssm_kernel(ul_ref, uc_ref, pc_ref, pr_ref, bc_ref, bt_ref, ct_ref, yl_ref, yc_ref,
                t_ref, w1_ref, w3_ref, sl_ref, sin_ref, *, n_ctx_chunks):
    L, H, P, LH = SSM_CHUNK, SSM_GROUP, SSM_STATE, SSM_LH
    ncl = ul_ref.shape[0]
    ctx_rows = uc_ref.shape[0]
    pc = pc_ref[...]
    pr = pr_ref[...]

    def powers(o):
        dt = jnp.exp(pc[:, o + 2:o + 3])
        zr = pc[:, o:o + 1] * dt
        zi = pc[:, o + 1:o + 2] * dt
        m = lax.broadcasted_iota(jnp.int32, (1, LANES), 1).astype(F32)
        mag = jnp.exp(zr * m)
        return mag * jnp.cos(zi * m), mag * jnp.sin(zi * m)

    def zoh(lr, li, ldt):
        dt = jnp.exp(ldt)
        mag = jnp.exp(lr * dt)
        nr = mag * jnp.cos(li * dt) - 1.0
        ni = mag * jnp.sin(li * dt)
        den = lr * lr + li * li
        return (nr * lr + ni * li) / den, (ni * lr - nr * li) / den

    def sel_mat(expo):
        w = expo.shape[1]
        return (lax.broadcasted_iota(jnp.int32, (LANES, w), 0) == expo).astype(BF16)

    def lane_tile(x, w):
        return jnp.concatenate([x] * (w // x.shape[1]), axis=1)

    def cmul(ar, ai, br, bi):
        return ar * br - ai * bi, ar * bi + ai * br

    tok_lh = lax.broadcasted_iota(jnp.int32, (1, LH), 1) // H
    lag2 = lax.broadcasted_iota(jnp.int32, (1, 2 * LH), 1) // H - (L - 1)

    w1_ref[...] = jnp.zeros_like(w1_ref)
    w3_ref[...] = jnp.zeros_like(w3_ref)
    klong = None
    a_chunk = []
    for dr in range(2):
        o = 3 * dr
        apr, api = powers(o)
        qr, qi = zoh(pr[o:o + 1, :], pr[o + 1:o + 2, :], pr[o + 2:o + 3, :])
        bbr_t, bbi_t = cmul(qr, qi, bt_ref[2 * dr], bt_ref[2 * dr + 1])
        lag = lag2 if dr == 0 else -lag2
        sel = sel_mat(lag)
        er, ei = _dot_sel([apr, api], sel)
        cr, ci = lane_tile(ct_ref[2 * dr], LANES), lane_tile(ct_ref[2 * dr + 1], LANES)
        gr, gi = cmul(er, ei, lane_tile(cr, 2 * LH), lane_tile(ci, 2 * LH))
        kd = _dot_f32(bbr_t, gr) - _dot_f32(bbi_t, gi)
        klong = kd if klong is None else klong + kd
        qrc, qic = zoh(pc[:, o:o + 1], pc[:, o + 1:o + 2], pc[:, o + 2:o + 3])
        bbr, bbi = cmul(qrc, qic, lane_tile(bc_ref[2 * dr], LANES), lane_tile(bc_ref[2 * dr + 1], LANES))
        sel = sel_mat((L - 1 - tok_lh) if dr == 0 else tok_lh)
        er, ei = _dot_sel([apr, api], sel)
        wr, wi = cmul(er, ei, lane_tile(bbr, LH), lane_tile(bbi, LH))
        w1_ref[(4 * dr) * P:(4 * dr + 1) * P, :] = wr.astype(BF16)
        w1_ref[(4 * dr + 2) * P:(4 * dr + 3) * P, :] = wi.astype(BF16)
        sel = sel_mat((tok_lh + 1) if dr == 0 else (L - tok_lh))
        er, ei = _dot_sel([apr, api], sel)
        gr, gi = cmul(er, ei, lane_tile(cr, LH), lane_tile(ci, LH))
        w3_ref[(4 * dr) * P:(4 * dr + 1) * P, :] = gr.astype(BF16)
        w3_ref[(4 * dr + 2) * P:(4 * dr + 3) * P, :] = (-gi).astype(BF16)
        dtr = jnp.exp(pr[o + 2:o + 3, :])
        mag = jnp.exp(pr[o:o + 1, :] * dtr * L)
        ang = pr[o + 1:o + 2, :] * dtr * L
        pad = jnp.zeros((1, LANES - P), F32)
        a_chunk.append((jnp.concatenate([mag * jnp.cos(ang), pad], axis=1),
                        jnp.concatenate([mag * jnp.sin(ang), pad], axis=1)))

    for j in range(L):
        start = (L - 1 - j) * H
        t_ref[j * H:(j + 1) * H, :] = klong[:, start:start + LH].astype(BF16)

    nt = (((1,), (1,)), ((), ()))
    ul = ul_ref[...]
    uc = uc_ref[...]
    sl_ref[0:ncl, :] = lax.dot_general(ul, w1_ref[...], nt, preferred_element_type=F32)
    sl_ref[ncl:, :] = lax.dot_general(uc, w1_ref[...], nt, preferred_element_type=F32)
    sin_ref[ncl + SUBLANES:, :] = jnp.zeros((ctx_rows - SUBLANES, 4 * LANES), F32)

    (afr, afi), (abr, abi) = a_chunk
    rid = lax.broadcasted_iota(jnp.int32, (SUBLANES, LANES), 0)

    def block_scan(carry, rf0, rb0, n_rows):
        fr, fi, br, bi = carry
        lf_r, lf_i = sl_ref[pl.ds(rf0, SUBLANES), 0:LANES], sl_ref[pl.ds(rf0, SUBLANES), LANES:2 * LANES]
        lb_r = sl_ref[pl.ds(rb0, SUBLANES), 2 * LANES:3 * LANES]
        lb_i = sl_ref[pl.ds(rb0, SUBLANES), 3 * LANES:4 * LANES]
        of_r = of_i = ob_r = ob_i = jnp.zeros((SUBLANES, LANES), F32)
        for k in range(n_rows):
            r = k
            of_r = jnp.where(rid == r, fr, of_r)
            of_i = jnp.where(rid == r, fi, of_i)
            fr, fi = (afr * fr - afi * fi + lf_r[r:r + 1, :], afr * fi + afi * fr + lf_i[r:r + 1, :])
            r = n_rows - 1 - k
            ob_r = jnp.where(rid == r, br, ob_r)
            ob_i = jnp.where(rid == r, bi, ob_i)
            br, bi = (abr * br - abi * bi + lb_r[r:r + 1, :], abr * bi + abi * br + lb_i[r:r + 1, :])
        sin_ref[pl.ds(rf0, SUBLANES), 0:LANES] = of_r
        sin_ref[pl.ds(rf0, SUBLANES), LANES:2 * LANES] = of_i
        sin_ref[pl.ds(rb0, SUBLANES), 2 * LANES:3 * LANES] = ob_r
        sin_ref[pl.ds(rb0, SUBLANES), 3 * LANES:4 * LANES] = ob_i
        return fr, fi, br, bi

    z = jnp.zeros((1, LANES), F32)
    carry = block_scan((z, z, z, z), ncl, ncl, n_ctx_chunks)
    nblk = ncl // SUBLANES

    def step(b, carry):
        rf0 = pl.multiple_of(b * SUBLANES, SUBLANES)
        rb0 = pl.multiple_of((nblk - 1 - b) * SUBLANES, SUBLANES)
        return block_scan(carry, rf0, rb0, SUBLANES)

    lax.fori_loop(0, nblk, step, carry)

    yl_ref[...] = (_dot(ul, t_ref[...]) + _dot(sin_ref[0:ncl, :].astype(BF16), w3_ref[...])).astype(BF16)
    yc_ref[...] = (_dot(uc, t_ref[...]) + _dot(sin_ref[ncl:, :].astype(BF16), w3_ref[...])).astype(BF16)


def _ssm_params(a_re, a_im, log_dt, b_re, b_im, c_re, c_im):
    depth, _, G, P = a_re.shape
    ldt = jnp.broadcast_to(log_dt[:, :, :, None], (depth, 2, G, P))
    zero = jnp.zeros((depth, G, P), F32)
    rows = jnp.stack([a_re[:, 0], a_im[:, 0], ldt[:, 0], a_re[:, 1], a_im[:, 1], ldt[:, 1], zero, zero],
                     axis=2)
    b_col = jnp.stack([b_re[:, 0], b_im[:, 0], b_re[:, 1], b_im[:, 1]], axis=2)
    c_t = jnp.stack([c_re[:, 0], c_im[:, 0], c_re[:, 1], c_im[:, 1]], axis=2).transpose(0, 1, 2, 4, 3)
    return dict(cols=rows.transpose(0, 1, 3, 2), rows=rows, b_col=b_col, b_t=b_col.transpose(0, 1, 2, 4, 3), c_t=c_t)


def _ssm(s_ctx, s_lat, sp, l):
    G, H, P, L, LH = SSM_GROUPS, SSM_GROUP, SSM_STATE, SSM_CHUNK, SSM_LH
    n_ctx, n_lat = s_ctx.shape[0], s_lat.shape[0]
    assert n_ctx % L == 0 and n_ctx // L <= SUBLANES and n_lat % (L * BF16_ROWS) == 0
    ncl, ncc = n_lat // L, n_ctx // L

    def chunked(s):
        return s.reshape(-1, L, G, H).transpose(2, 0, 1, 3).reshape(G, -1, LH)

    ul = chunked(s_lat)
    uc = jnp.pad(chunked(s_ctx), ((0, 0), (0, BF16_ROWS - ncc), (0, 0)))

    def gspec(shape):
        nd = len(shape)
        return pl.BlockSpec((None,) + shape, lambda g: (g,) + (0,) * nd)

    def lspec(shape):
        nd = len(shape)
        return pl.BlockSpec((None, None) + shape, lambda g: (l, g) + (0,) * nd)

    yl, yc = pl.pallas_call(
        functools.partial(_ssm_kernel, n_ctx_chunks=ncc),
        grid=(G,),
        in_specs=[gspec((ncl, LH)), gspec((BF16_ROWS, LH)), lspec((P, 8)), lspec((8, P)), lspec((4, P, H)),
                  lspec((4, H, P)), lspec((4, P, H))],
        out_specs=[gspec((ncl, LH)), gspec((BF16_ROWS, LH))],
        out_shape=[jax.ShapeDtypeStruct((G, ncl, LH), BF16), jax.ShapeDtypeStruct((G, BF16_ROWS, LH), BF16)],
        scratch_shapes=[
            pltpu.VMEM((LH, LH), BF16),
            pltpu.VMEM((8 * P, LH), BF16),
            pltpu.VMEM((8 * P, LH), BF16),
            pltpu.VMEM((ncl + BF16_ROWS, 4 * LANES), F32),
            pltpu.VMEM((ncl + BF16_ROWS, 4 * LANES), F32),
        ],
        compiler_params=_params("arbitrary"),
        name="ssm",
    )(ul, uc, sp["cols"], sp["rows"], sp["b_col"], sp["b_t"], sp["c_t"])

    def unchunked(y):
        return y.reshape(G, -1, L, H).transpose(1, 2, 0, 3).reshape(-1, G * H)

    return unchunked(yc[:, :ncc]), unchunked(yl)


def _attn_kernel(q_ref, k_ref, v_ref, o_ref, s_ref, m_ref, l_ref, acc_ref):
    nch = k_ref.shape[0]
    tq = q_ref.shape[2]
    width = min(ATTN_STREAM_W, tq)
    streams = [(h, slice(u * width, (u + 1) * width)) for h in range(KV_REP) for u in range(tq // width)]
    m_ref[...] = jnp.full_like(m_ref, -jnp.inf)
    l_ref[...] = jnp.zeros_like(l_ref)
    acc_ref[...] = jnp.zeros_like(acc_ref)

    def scores(c, slot, st):
        h, lanes = st
        s = _dot(k_ref[c], q_ref[h, :, lanes])
        s_ref[slot, h, :, lanes] = s
        return jnp.max(s, axis=0, keepdims=True)

    def consume(c, slot, st, mx):
        h, lanes = st
        m_old = m_ref[h, :, lanes]
        m_new = jnp.maximum(m_old, mx)
        alpha = jnp.exp2(m_old - m_new)
        p = jnp.exp2(s_ref[slot, h, :, lanes] - m_new)
        l_ref[h, :, lanes] = alpha * l_ref[h, :, lanes] + jnp.sum(p, axis=0, keepdims=True)
        acc_ref[h, :, lanes] = alpha * acc_ref[h, :, lanes] + _dot(v_ref[c], p.astype(BF16))
        m_ref[h, :, lanes] = m_new

    def half(c, slot, mx):
        new = []
        for i, st in enumerate(streams):
            new.append(scores(c + 1, 1 - slot, st))
            consume(c, slot, st, mx[i])
        return tuple(new)

    def body(d, mx):
        for u in range(ATTN_UNROLL):
            mx = half(ATTN_UNROLL * d + u, u % 2, mx)
        return mx

    mx = tuple(scores(0, 0, st) for st in streams)
    nloop = (nch - 1) // ATTN_UNROLL
    mx = lax.fori_loop(0, nloop, body, mx)
    c = ATTN_UNROLL * nloop
    while c < nch - 1:
        mx = half(c, c % 2, mx)
        c += 1
    for i, st in enumerate(streams):
        consume(c, c % 2, st, mx[i])
    for h in range(KV_REP):
        o_ref[h] = (acc_ref[h] / l_ref[h]).astype(BF16)


def _key_chunk(nk):
    for tk in (640, 512, 256):
        if nk % tk == 0:
            return tk
    raise ValueError(f"key count {nk} must be a multiple of 256")


def _attention(qp, k, vt):
    nt, _, _, tq = qp.shape
    nk = k.shape[0]
    tk = _key_chunk(nk)
    nch = nk // tk
    kr = k.reshape(nch, tk, KV_W)
    vr = vt.reshape(N_KV_HEADS, HEAD_DIM, nch, tk).transpose(0, 2, 1, 3)
    return pl.pallas_call(
        _attn_kernel,
        grid=(N_KV_HEADS, nt),
        in_specs=[
            pl.BlockSpec((None, KV_REP, KV_W, tq), lambda j, i: (i, j, 0, 0)),
            pl.BlockSpec((nch, tk, KV_W), lambda j, i: (0, 0, 0)),
            pl.BlockSpec((None, nch, HEAD_DIM, tk), lambda j, i: (j, 0, 0, 0)),
        ],
        out_specs=pl.BlockSpec((None, KV_REP, HEAD_DIM, tq), lambda j, i: (i, j, 0, 0)),
        out_shape=jax.ShapeDtypeStruct((nt, N_Q_HEADS, HEAD_DIM, tq), BF16),
        scratch_shapes=[
            pltpu.VMEM((2, KV_REP, tk, tq), F32),
            pltpu.VMEM((KV_REP, 1, tq), F32),
            pltpu.VMEM((KV_REP, 1, tq), F32),
            pltpu.VMEM((KV_REP, HEAD_DIM, tq), F32),
        ],
        compiler_params=_params("arbitrary", "arbitrary"),
        name="attention",
    )(qp, kr, vr)


def _merge_kernel(x_ref, a_ref, ap_ref, an_ref, s_ref, y_ref, o_ref, mod_ref, g_ref, wg_ref, pw_ref, ps_ref,
                  po_ref, d_ref, wglu_ref, wao_ref, wout_ref, out_ref, ext_ref, *, n_rows):
    tm = x_ref.shape[0]
    i = pl.program_id(0)
    x = x_ref[...]
    h = _modulate(x, g_ref[1:2, :], mod_ref[0:1, :], mod_ref[1:2, :]).astype(BF16)
    gates = jax.nn.sigmoid(_dot(h, wg_ref[:, N_PROJ:]))

    a = a_ref[...]
    ext_ref[0:POOL_HALO, :] = jnp.where(i > 0, ap_ref[...], 0.0)
    ext_ref[POOL_HALO:POOL_HALO + tm, :] = a
    ext_ref[POOL_HALO + tm:, :] = jnp.where(i < pl.num_programs(0) - 1, an_ref[...], 0.0)
    e = ext_ref[...]
    r = tm + 2 * POOL_HALO
    s2 = e + pltpu.roll(e, 1, 0)
    s4 = pltpu.roll(s2, 1, 0) + pltpu.roll(s2, r - 1, 0)
    s8 = pltpu.roll(s4, 2, 0) + pltpu.roll(s4, r - 2, 0)
    s16 = pltpu.roll(s8, 4, 0) + pltpu.roll(s8, r - 4, 0)
    lane_grp = lax.broadcasted_iota(jnp.int32, (1, POOL_W), 1) // POOL_GROUP
    sums = jnp.where(lane_grp == 0, s2, jnp.where(lane_grp == 1, s4, jnp.where(lane_grp == 2, s8, s16)))
    sums = sums[POOL_HALO:POOL_HALO + tm, :]
    half_w = jnp.left_shift(1, lane_grp)
    row = i * tm + lax.broadcasted_iota(jnp.int32, (tm, 1), 0)
    cnt = jnp.minimum(row + half_w, n_rows) - jnp.maximum(row - half_w, 0)
    pooled = sums / cnt.astype(F32) - a
    ya = _dot(pooled.astype(BF16), pw_ref[...]) * ps_ref[...]
    br_a = _dot(ya.astype(BF16), po_ref[...])

    y = s_ref[...] * d_ref[...] + y_ref[...].astype(F32)
    z = _dot(jax.nn.gelu(y).astype(BF16), wglu_ref[...])
    br_b = z[:, :D_MODEL] * jax.nn.sigmoid(z[:, D_MODEL:])

    o_t = o_ref[...].reshape(Q_W, tm)
    br_c = lax.dot_general(o_t, wao_ref[...], (((0,), (0,)), ((), ())), preferred_element_type=F32)

    mixed = (gates[:, :D_MODEL] * br_a + gates[:, D_MODEL:2 * D_MODEL] * br_b
             + gates[:, 2 * D_MODEL:] * br_c)
    out_ref[...] = x + mod_ref[2:3, :] * _dot(mixed.astype(BF16), wout_ref[...])


def _merge(x, a, s, y_ssm, o_attn, p, l, stream):
    n, d = x.shape
    tm = _row_tile(n)
    nb = tm // POOL_HALO
    last = n // POOL_HALO - 1
    row = lambda w: pl.BlockSpec((tm, w), lambda i: (i, 0))
    return pl.pallas_call(
        functools.partial(_merge_kernel, n_rows=n),
        grid=(n // tm,),
        in_specs=[
            row(d), row(POOL_W),
            pl.BlockSpec((POOL_HALO, POOL_W), lambda i: (jnp.maximum(i * nb - 1, 0), 0)),
            pl.BlockSpec((POOL_HALO, POOL_W), lambda i: (jnp.minimum((i + 1) * nb, last), 0)),
            row(SSM_W), row(SSM_W),
            pl.BlockSpec((None, N_Q_HEADS, HEAD_DIM, tm), lambda i: (i, 0, 0, 0)),
            _pick((None, None, None, 3, d), l, stream, 1, 0, 0),
            _pick((None, 3, d), l, 0, 0),
            _pick((None,) + p["w_in"].shape[1:], l, 0, 0),
            _pick((None, POOL_W, POOL_W), l, 0, 0),
            _pick((None, 1, POOL_W), l, 0, 0),
            _pick((None, POOL_W, d), l, 0, 0),
            _pick((None, 1, SSM_W), l, 0, 0),
            _pick((None, SSM_W, 2 * d), l, 0, 0),
            _pick((None, Q_W, d), l, 0, 0),
            _pick((None, d, d), l, 0, 0),
        ],
        out_specs=row(d),
        out_shape=jax.ShapeDtypeStruct((n, d), F32),
        scratch_shapes=[pltpu.VMEM((tm + 2 * POOL_HALO, POOL_W), F32)],
        compiler_params=_params("arbitrary"),
        name="merge",
    )(x, a, a, a, s, y_ssm, o_attn, p["mods"], p["norm_g"], p["w_in"], p["pool_bd"], p["pool_scale"],
      p["pool_out"], p["ssm_d"], p["w_glu"], p["w_ao"], p["w_out"])


def kernel(x, c, ctx, c_ctx, mod_w, mod_b, norm_g, ffn_w13, ffn_w2, w_in, pool_w, pool_scale, pool_out, ssm_a_re, ssm_a_im, ssm_log_dt, ssm_b_re, ssm_b_im, ssm_c_re, ssm_c_im, ssm_d, ssm_glu_w, q_norm_g, k_norm_g, attn_out, w_out, final_norm_g):
    batch, n, d = x.shape
    depth = mod_w.shape[0]
    assert batch == 1 and d == D_MODEL and n % GRID_W == 0
    xl = x[0]
    xc = ctx[0]

    cond8 = jnp.concatenate([c, c_ctx[None], jnp.zeros((SUBLANES - 2, d), F32)], axis=0)
    n_groups = POOL_W // POOL_GROUP
    pool_bd = (jnp.eye(n_groups, dtype=F32)[None, :, None, :, None] * pool_w[:, :, :, None, :]
               ).reshape(depth, POOL_W, POOL_W)
    gn = jnp.stack([q_norm_g * Q_SCALE, k_norm_g], axis=1)
    p = dict(
        mods=_adaln(cond8, mod_w, mod_b).reshape(depth, SUBLANES, 3, 3, d),
        norm_g=norm_g,
        w13=ffn_w13.astype(BF16),
        w2=ffn_w2.astype(BF16),
        w_in=w_in.astype(BF16),
        w_t=w_in[:, :, POOL_W + SSM_W:N_PROJ].transpose(0, 2, 1).astype(BF16),
        gn=jnp.broadcast_to(gn[:, :, :, None], (depth, 2, HEAD_DIM, ROW_TILE)),
        pool_bd=pool_bd.astype(BF16),
        pool_scale=pool_scale.reshape(depth, 1, POOL_W),
        pool_out=pool_out.astype(BF16),
        ssm_d=ssm_d.reshape(depth, 1, SSM_W),
        w_glu=ssm_glu_w.astype(BF16),
        w_ao=attn_out.astype(BF16),
        w_out=w_out.astype(BF16),
    )
    sp = _ssm_params(ssm_a_re, ssm_a_im, ssm_log_dt, ssm_b_re, ssm_b_im, ssm_c_re, ssm_c_im)
    rope_tab = _rope_table(n)
    LAT, CTX = 0, 1

    for l in range(depth):
        with_ctx_out = l < depth - 1
        xl = _ffn(xl, p, l, LAT, 0)
        xc = _ffn(xc, p, l, CTX, 0)
        a_l, s_l, sb_l, q_l, k_l, v_l = _inproj(xl, p, l, LAT, rope_tab)
        a_c, s_c, sb_c, q_c, k_c, v_c = _inproj(xc, p, l, CTX)
        y_c, y_l = _ssm(sb_c, sb_l, sp, l)
        o_l = _attention(q_l, jnp.concatenate([k_l, k_c], axis=0), jnp.concatenate([v_l, v_c], axis=2))
        xl = _merge(xl, a_l, s_l, y_l, o_l, p, l, LAT)
        if with_ctx_out:
            o_c = _attention(q_c, k_c, v_c)
            xc = _merge(xc, a_c, s_c, y_c, o_c, p, l, CTX)
        last = l == depth - 1
        xl = _ffn(xl, p, l, LAT, 1, final_norm_g if last else None)
        if with_ctx_out:
            xc = _ffn(xc, p, l, CTX, 1)
    return xl[None]
```
